```python
import math
import numpy as np
import jax
import jax.numpy as jnp
from jax import lax

D_MODEL = 1024
BATCH = 8
SEQ = 2048
DEPTH = 4
DEC_BATCH = 128
DEC_SEQ = 8
PAST_LEN = 16384
PAGE_SIZE = 128

D_MIX = D_MODEL
D_CONV = D_MIX // 4
CONV_WIDTH = 31
N_GDN_HEADS = 4
GDN_HEAD_DIM = D_MIX // (2 * N_GDN_HEADS)
D_GDN = N_GDN_HEADS * GDN_HEAD_DIM
GDN_CONV = 4
GDN_CHUNK = 64
D_LRU = D_MIX - D_CONV - D_GDN
N_LRU_BLOCKS = 4
LRU_BLOCK = D_LRU // N_LRU_BLOCKS
LRU_CONV = 4
LRU_C = 8.0
D_FF = 2816
N_EXPERTS = 8
TOP_K = 2
D_FF_EXPERT = 1408
MOE_BLOCK = 128
N_DENSE_LAYERS = (DEPTH + 1) // 2
N_MOE_LAYERS = DEPTH // 2
EPS = 1e-6
IN_SIZES = (D_CONV, D_CONV, 3 * D_GDN, D_GDN, N_GDN_HEADS, N_GDN_HEADS, D_LRU, D_LRU)
D_IN = sum(IN_SIZES)

kernel_name = 'hybrid_conv_gdn_rglru_adaln_step'


def rmsnorm(x, g):
    xf = x.astype(jnp.float32)
    y = xf * lax.rsqrt(jnp.mean(xf * xf, axis=-1, keepdims=True) + EPS)
    return (y * g.astype(jnp.float32)).astype(x.dtype)


def layernorm(x, g, b):
    xf = x.astype(jnp.float32)
    mu = jnp.mean(xf, axis=-1, keepdims=True)
    var = jnp.mean(jnp.square(xf - mu), axis=-1, keepdims=True)
    y = (xf - mu) * lax.rsqrt(var + EPS) * g.astype(jnp.float32) + b.astype(jnp.float32)
    return y.astype(x.dtype)


def l2norm(x):
    xf = x.astype(jnp.float32)
    return xf * lax.rsqrt(jnp.sum(xf * xf, axis=-1, keepdims=True) + EPS)


def causal_dwconv(x, buf, w):
    width, ch = w.shape
    xp = jnp.concatenate([buf.astype(x.dtype), x], axis=1)
    y = lax.conv_general_dilated(xp, w[:, None, :].astype(x.dtype), window_strides=(1,), padding='VALID',
                                 dimension_numbers=('NWC', 'WIO', 'NWC'), feature_group_count=ch)
    return y, xp[:, xp.shape[1] - (width - 1):]


def swiglu(x, wg, wu, wd):
    return (jax.nn.silu(x @ wg) * (x @ wu)) @ wd


def gated_delta_chunked(q, k, v, g, beta, s0):
    bsz, t, nh, dk = q.shape
    cs = min(GDN_CHUNK, t)
    nc = -(-t // cs)
    pad = nc * cs - t

    def blocks(a):
        a = a.astype(jnp.float32)
        a = jnp.pad(a, [(0, 0), (0, pad)] + [(0, 0)] * (a.ndim - 2))
        a = jnp.moveaxis(a, 2, 1)
        return a.reshape((bsz, nh, nc, cs) + a.shape[3:])

    q, k, v, g, beta = (blocks(a) for a in (q, k, v, g, beta))
    g = jnp.cumsum(g, axis=-1)
    pos = jnp.arange(cs)
    incl = pos[:, None] >= pos[None, :]
    strict = pos[:, None] > pos[None, :]
    decay = jnp.exp(jnp.where(incl, g[..., :, None] - g[..., None, :], -jnp.inf))
    kb = k * beta[..., None]
    lower = jnp.where(strict, jnp.einsum('bhncd,bhnsd->bhncs', kb, k) * decay, 0.0)
    eye = jnp.eye(cs, dtype=jnp.float32)
    tinv = lax.linalg.triangular_solve(lower + eye, jnp.broadcast_to(eye, lower.shape),
                                       left_side=True, lower=True, unit_diagonal=True)
    u = jnp.einsum('bhncs,bhnse->bhnce', tinv, v * beta[..., None])
    w = jnp.einsum('bhncs,bhnsd->bhncd', tinv, kb * jnp.exp(g)[..., None])
    qk = jnp.einsum('bhncd,bhnsd->bhncs', q, k) * decay
    g_last = g[..., -1]

    def chunk_step(s, xs):
        q_c, k_c, u_c, w_c, g_c, gl_c, qk_c = xs
        v_new = u_c - jnp.einsum('bhcd,bhde->bhce', w_c, s)
        o_c = (jnp.einsum('bhcd,bhde->bhce', q_c * jnp.exp(g_c)[..., None], s)
               + jnp.einsum('bhcs,bhse->bhce', qk_c, v_new))
        k_dec = k_c * jnp.exp(gl_c[..., None] - g_c)[..., None]
        s = s * jnp.exp(gl_c)[..., None, None] + jnp.einsum('bhcd,bhce->bhde', k_dec, v_new)
        return s, o_c

    xs = tuple(jnp.moveaxis(a, 2, 0) for a in (q, k, u, w, g, g_last, qk))
    s_fin, o = lax.scan(chunk_step, s0.astype(jnp.float32), xs)
    o = jnp.moveaxis(o, 0, 2).reshape(bsz, nh, nc * cs, -1)[:, :, :t]
    return jnp.moveaxis(o, 1, 2), s_fin


def rg_lru(x, h0, w_a, b_a, w_i, b_i, lam):
    bsz, t, _ = x.shape
    xb = x.reshape(bsz, t, N_LRU_BLOCKS, LRU_BLOCK)
    r = jax.nn.sigmoid((jnp.einsum('btni,nij->btnj', xb, w_a).reshape(bsz, t, D_LRU) + b_a).astype(jnp.float32))
    i = jax.nn.sigmoid((jnp.einsum('btni,nij->btnj', xb, w_i).reshape(bsz, t, D_LRU) + b_i).astype(jnp.float32))
    log_a = -LRU_C * r * jax.nn.softplus(-lam.astype(jnp.float32))
    a = jnp.exp(log_a)
    b = jnp.sqrt(-jnp.expm1(2.0 * log_a)) * (i * x.astype(jnp.float32))
    b = b.at[:, 0].add(a[:, 0] * h0.astype(jnp.float32))

    def combine(lhs, rhs):
        return lhs[0] * rhs[0], rhs[0] * lhs[1] + rhs[1]

    _, h = lax.associative_scan(combine, (a, b), axis=1)
    return h.astype(x.dtype), h[:, -1]


def moe_ffn(h, w_router, b_router, wg, wu, wd):
    bsz, t, d = h.shape
    x = h.reshape(-1, d)
    n_tok = x.shape[0]
    logits = (x @ w_router).astype(jnp.float32) + b_router.astype(jnp.float32)
    top_val, top_idx = lax.top_k(logits, TOP_K)
    gates = jax.nn.softmax(top_val, axis=-1)
    n_asg = n_tok * TOP_K
    e_flat = top_idx.reshape(-1)
    tok = jnp.arange(n_asg) // TOP_K
    order = jnp.argsort(e_flat)
    e_sorted = e_flat[order]
    counts = jnp.bincount(e_flat, length=N_EXPERTS)
    padded = (counts + MOE_BLOCK - 1) // MOE_BLOCK * MOE_BLOCK
    start = jnp.cumsum(counts) - counts
    pends = jnp.cumsum(padded)
    pstart = pends - padded
    dest = pstart[e_sorted] + jnp.arange(n_asg) - start[e_sorted]
    n_blocks = -(-n_asg // MOE_BLOCK) + N_EXPERTS
    rows = n_blocks * MOE_BLOCK
    xs = jnp.zeros((rows, d), x.dtype).at[dest].set(x[tok[order]])
    block_e = jnp.minimum(jnp.searchsorted(pends, jnp.arange(n_blocks) * MOE_BLOCK, side='right'), N_EXPERTS - 1)

    def expert_block(args):
        xblk, e = args
        return swiglu(xblk, wg[e], wu[e], wd[e])

    ys = lax.map(expert_block, (xs.reshape(n_blocks, MOE_BLOCK, d), block_e)).reshape(rows, d)
    y_asg = jnp.zeros((n_asg, d), ys.dtype).at[order].set(ys[dest])
    y = jnp.einsum('nk,nkd->nd', gates.astype(y_asg.dtype), y_asg.reshape(n_tok, TOP_K, d))
    return y.reshape(bsz, t, d)


def hybrid_layer(l, x, c_act, s_conv_a, s_qkv, s_delta, s_conv_lru, s_lru, p):
    bsz, t, _ = x.shape
    f32 = jnp.float32
    mod = (c_act @ p['w_ada'][l] + p['b_ada'][l])[:, None, :]
    sh1, sc1, g1, sh2, sc2, g2 = jnp.split(mod, 6, axis=-1)
    h = rmsnorm(x, p['norm1'][l]) * (1 + sc1) + sh1
    proj = h @ p['w_in'][l]
    a_val, a_gate, qkv, z, b_raw, a_raw, lru_x, lru_g = jnp.split(proj, np.cumsum(IN_SIZES)[:-1].tolist(), axis=-1)

    u = a_val * jax.nn.sigmoid(a_gate)
    u, new_conv_a = causal_dwconv(u, s_conv_a, p['conv_a_w'][l])
    y_a = jax.nn.silu(layernorm(u + p['conv_a_b'][l], p['ln_a_g'][l], p['ln_a_b'][l]))

    qkv_c, new_qkv = causal_dwconv(qkv, s_qkv, p['conv_qkv_w'][l])
    q, k, v = jnp.split(jax.nn.silu(qkv_c).reshape(bsz, t, 3 * N_GDN_HEADS, GDN_HEAD_DIM), 3, axis=2)
    q = l2norm(q) * GDN_HEAD_DIM ** -0.5
    k = l2norm(k)
    beta = jax.nn.sigmoid(b_raw.astype(f32))
    g = -jnp.exp(p['gdn_a_log'][l].astype(f32)) * jax.nn.softplus(a_raw.astype(f32) + p['gdn_dt_bias'][l].astype(f32))
    o, new_delta = gated_delta_chunked(q, k, v, g, beta, s_delta)
    o = rmsnorm(o, p['gdn_norm'][l]) * jax.nn.silu(z.reshape(bsz, t, N_GDN_HEADS, GDN_HEAD_DIM).astype(f32))
    y_b = o.reshape(bsz, t, D_GDN).astype(x.dtype)

    xc, new_conv_lru = causal_dwconv(lru_x, s_conv_lru, p['conv_lru_w'][l])
    hs, new_lru = rg_lru(xc + p['conv_lru_b'][l], s_lru, p['lru_w_a'][l], p['lru_b_a'][l],
                         p['lru_w_i'][l], p['lru_b_i'][l], p['lru_lambda'][l])
    y_c = rmsnorm(hs * jax.nn.gelu(lru_g), p['lru_norm'][l])

    mix = jnp.concatenate([y_a, y_b, y_c], axis=-1) @ p['w_out'][l]
    x = x + g1 * mix

    h = rmsnorm(x, p['norm2'][l]) * (1 + sc2) + sh2
    if l % 2 == 0:
        j = l // 2
        f = swiglu(h, p['ffn_wg'][j], p['ffn_wu'][j], p['ffn_wd'][j])
    else:
        j = l // 2
        f = moe_ffn(h, p['router_w'][j], p['router_b'][j], p['moe_wg'][j], p['moe_wu'][j], p['moe_wd'][j])
    x = x + g2 * f
    return x, (new_conv_a, new_qkv, new_delta.astype(s_delta.dtype), new_conv_lru, new_lru.astype(s_lru.dtype))


def run_trunk(x, c, s_conv_a, s_qkv, s_delta, s_conv_lru, s_lru, p):
    c_act = jax.nn.silu(c)
    new = []
    for l in range(DEPTH):
        x, st = hybrid_layer(l, x, c_act, s_conv_a[l], s_qkv[l], s_delta[l], s_conv_lru[l], s_lru[l], p)
        new.append(st)
    y = rmsnorm(x, p['final_norm'])
    stacked = [jnp.stack(parts, axis=0) for parts in zip(*new)]
    return y, stacked


def setup_inputs(seed: int = 0) -> dict:
    key = jax.random.key(seed)
    keys = iter(jax.random.split(key, 64))

    def nrm(shape, scale):
        return jax.random.normal(next(keys), shape, jnp.float32) * scale

    def gain(shape):
        return 1.0 + nrm(shape, 0.02)

    dsc = D_MODEL ** -0.5
    lru_a0 = jax.random.uniform(next(keys), (DEPTH, D_LRU), jnp.float32, 0.9, 0.999)
    lru_p = lru_a0 ** (1.0 / LRU_C)
    dt = jnp.exp(jax.random.uniform(next(keys), (DEPTH, N_GDN_HEADS), jnp.float32, math.log(1e-3), math.log(0.1)))
    a_init = jax.random.uniform(next(keys), (DEPTH, N_GDN_HEADS), jnp.float32, 1.0, 16.0)
    inp = {}
    inp['x_prompt'] = nrm((BATCH, SEQ, D_MODEL), 1.0)
    inp['x_sample'] = nrm((DEC_BATCH, DEC_SEQ, D_MODEL), 1.0)
    inp['c_prompt'] = nrm((BATCH, D_MODEL), 1.0)
    inp['c_sample'] = nrm((DEC_BATCH, D_MODEL), 1.0)
    inp['state_conv_a'] = nrm((DEPTH, DEC_BATCH, CONV_WIDTH - 1, D_CONV), 0.5)
    inp['state_conv_qkv'] = nrm((DEPTH, DEC_BATCH, GDN_CONV - 1, 3 * D_GDN), 1.0)
    inp['state_delta'] = nrm((DEPTH, DEC_BATCH, N_GDN_HEADS, GDN_HEAD_DIM, GDN_HEAD_DIM), 0.1)
    inp['state_conv_lru'] = nrm((DEPTH, DEC_BATCH, LRU_CONV - 1, D_LRU), 1.0)
    inp['state_lru'] = nrm((DEPTH, DEC_BATCH, D_LRU), 0.5)
    inp['w_ada'] = nrm((DEPTH, D_MODEL, 6 * D_MODEL), dsc)
    inp['b_ada'] = nrm((DEPTH, 6 * D_MODEL), 0.02)
    inp['norm1'] = gain((DEPTH, D_MODEL))
    inp['norm2'] = gain((DEPTH, D_MODEL))
    inp['w_in'] = nrm((DEPTH, D_MODEL, D_IN), dsc)
    inp['w_out'] = nrm((DEPTH, D_MIX, D_MODEL), D_MIX ** -0.5)
    inp['conv_a_w'] = nrm((DEPTH, CONV_WIDTH, D_CONV), CONV_WIDTH ** -0.5)
    inp['conv_a_b'] = nrm((DEPTH, D_CONV), 0.02)
    inp['ln_a_g'] = gain((DEPTH, D_CONV))
    inp['ln_a_b'] = nrm((DEPTH, D_CONV), 0.02)
    inp['conv_qkv_w'] = nrm((DEPTH, GDN_CONV, 3 * D_GDN), GDN_CONV ** -0.5)
    inp['gdn_a_log'] = jnp.log(a_init)
    inp['gdn_dt_bias'] = dt + jnp.log(-jnp.expm1(-dt))
    inp['gdn_norm'] = gain((DEPTH, GDN_HEAD_DIM))
    inp['conv_lru_w'] = nrm((DEPTH, LRU_CONV, D_LRU), LRU_CONV ** -0.5)
    inp['conv_lru_b'] = nrm((DEPTH, D_LRU), 0.02)
    inp['lru_w_a'] = nrm((DEPTH, N_LRU_BLOCKS, LRU_BLOCK, LRU_BLOCK), LRU_BLOCK ** -0.5)
    inp['lru_b_a'] = nrm((DEPTH, D_LRU), 0.02)
    inp['lru_w_i'] = nrm((DEPTH, N_LRU_BLOCKS, LRU_BLOCK, LRU_BLOCK), LRU_BLOCK ** -0.5)
    inp['lru_b_i'] = nrm((DEPTH, D_LRU), 0.02)
    inp['lru_lambda'] = jnp.log(lru_p) - jnp.log1p(-lru_p)
    inp['lru_norm'] = gain((DEPTH, D_LRU))
    inp['ffn_wg'] = nrm((N_DENSE_LAYERS, D_MODEL, D_FF), dsc)
    inp['ffn_wu'] = nrm((N_DENSE_LAYERS, D_MODEL, D_FF), dsc)
    inp['ffn_wd'] = nrm((N_DENSE_LAYERS, D_FF, D_MODEL), D_FF ** -0.5)
    inp['router_w'] = nrm((N_MOE_LAYERS, D_MODEL, N_EXPERTS), dsc)
    inp['router_b'] = nrm((N_MOE_LAYERS, N_EXPERTS), 0.01)
    inp['moe_wg'] = nrm((N_MOE_LAYERS, N_EXPERTS, D_MODEL, D_FF_EXPERT), dsc)
    inp['moe_wu'] = nrm((N_MOE_LAYERS, N_EXPERTS, D_MODEL, D_FF_EXPERT), dsc)
    inp['moe_wd'] = nrm((N_MOE_LAYERS, N_EXPERTS, D_FF_EXPERT, D_MODEL), D_FF_EXPERT ** -0.5)
    inp['final_norm'] = gain((D_MODEL,))
    return inp


def reference(x_prompt, x_sample, c_prompt, c_sample, state_conv_a, state_conv_qkv, state_delta,
              state_conv_lru, state_lru, w_ada, b_ada, norm1, norm2, w_in, w_out, conv_a_w, conv_a_b,
              ln_a_g, ln_a_b, conv_qkv_w, gdn_a_log, gdn_dt_bias, gdn_norm, conv_lru_w, conv_lru_b,
              lru_w_a, lru_b_a, lru_w_i, lru_b_i, lru_lambda, lru_norm, ffn_wg, ffn_wu, ffn_wd,
              router_w, router_b, moe_wg, moe_wu, moe_wd, final_norm):
    p = dict(w_ada=w_ada, b_ada=b_ada, norm1=norm1, norm2=norm2, w_in=w_in, w_out=w_out,
             conv_a_w=conv_a_w, conv_a_b=conv_a_b, ln_a_g=ln_a_g, ln_a_b=ln_a_b, conv_qkv_w=conv_qkv_w,
             gdn_a_log=gdn_a_log, gdn_dt_bias=gdn_dt_bias, gdn_norm=gdn_norm, conv_lru_w=conv_lru_w,
             conv_lru_b=conv_lru_b, lru_w_a=lru_w_a, lru_b_a=lru_b_a, lru_w_i=lru_w_i, lru_b_i=lru_b_i,
             lru_lambda=lru_lambda, lru_norm=lru_norm, ffn_wg=ffn_wg, ffn_wu=ffn_wu, ffn_wd=ffn_wd,
             router_w=router_w, router_b=router_b, moe_wg=moe_wg, moe_wu=moe_wu, moe_wd=moe_wd,
             final_norm=final_norm)
    bp = x_prompt.shape[0]

    def zeros_like_state(s):
        return jnp.zeros((DEPTH, bp) + s.shape[2:], x_prompt.dtype)

    y_prompt, (conv_a_p, conv_qkv_p, delta_p, conv_lru_p, lru_p) = run_trunk(
        x_prompt, c_prompt, zeros_like_state(state_conv_a), zeros_like_state(state_conv_qkv),
        zeros_like_state(state_delta), zeros_like_state(state_conv_lru), zeros_like_state(state_lru), p)
    y_sample, (conv_a_s, conv_qkv_s, delta_s, conv_lru_s, lru_s) = run_trunk(
        x_sample, c_sample, state_conv_a, state_conv_qkv, state_delta, state_conv_lru, state_lru, p)
    return (y_prompt, y_sample, conv_a_p, conv_qkv_p, delta_p, conv_lru_p, lru_p,
            conv_a_s, conv_qkv_s, delta_s, conv_lru_s, lru_s)
```

```python
import functools

import jax
import jax.numpy as jnp
from jax import lax
from jax.experimental import pallas as pl
from jax.experimental.pallas import tpu as pltpu

F32 = jnp.float32
BF16 = jnp.bfloat16
EPS = 1e-6

V7X_VMEM_LIMIT_BYTES = 56 * 1024 * 1024
LANES = 128
TOKEN_TILE = 512
MOE_ROWS = 256
N_EXPERTS = 8
GDN_CHUNK = 64
LRU_C = 8.0
NEG_BIG = -1e30


def _cparams(sem):
    return pltpu.CompilerParams(dimension_semantics=sem, vmem_limit_bytes=V7X_VMEM_LIMIT_BYTES)


def _sigmoid(x):
    return jax.nn.sigmoid(x)


def _silu(x):
    return x * jax.nn.sigmoid(x)


def _softplus(x):
    return jnp.maximum(x, 0.0) + jnp.log1p(jnp.exp(-jnp.abs(x)))


def _dot(a, b):
    return jnp.dot(a.astype(BF16), b.astype(BF16), preferred_element_type=F32)


def _dot_nt(a, b):
    return lax.dot_general(a.astype(BF16), b.astype(BF16), (((1,), (1,)), ((), ())),
                           preferred_element_type=F32)


def _split2(x):
    hi = x.astype(BF16)
    lo = (x - hi.astype(F32)).astype(BF16)
    return hi, lo


def _split3(x):
    hi = x.astype(BF16)
    r = x - hi.astype(F32)
    mid = r.astype(BF16)
    lo = (r - mid.astype(F32)).astype(BF16)
    return hi, mid, lo


def _dot_hp(a, b):
    ah, al = _split2(a)
    bh, bl = _split2(b)
    d = functools.partial(jnp.dot, preferred_element_type=F32)
    return d(ah, bh) + (d(ah, bl) + d(al, bh))


def _dot_exact_lhs(a_bf16, b):
    d = functools.partial(jnp.dot, preferred_element_type=F32)
    h, m, l = _split3(b)
    return d(a_bf16, h) + (d(a_bf16, m) + d(a_bf16, l))


def _tok_tiles(B, T, target=TOKEN_TILE):
    if T >= target:
        assert T % target == 0
        return 1, target
    bb = max(1, min(B, target // T))
    assert B % bb == 0
    return bb, T


def _ada_kernel(c_ref, w_ref, b_ref, o_ref):
    c = c_ref[...]
    o_ref[...] = _dot(_silu(c), w_ref[...]) + b_ref[...]


def _ada(c_all, w_ada, b_ada):
    L, D, D6 = w_ada.shape
    Bc = c_all.shape[0]
    tn = 1536
    return pl.pallas_call(
        _ada_kernel, grid=(L, D6 // tn),
        in_specs=[pl.BlockSpec((Bc, D), lambda l, j: (0, 0)),
                  pl.BlockSpec((None, D, tn), lambda l, j: (l, 0, j)),
                  pl.BlockSpec((None, 1, tn), lambda l, j: (l, 0, j))],
        out_specs=pl.BlockSpec((None, Bc, tn), lambda l, j: (l, 0, j)),
        out_shape=jax.ShapeDtypeStruct((L, Bc, D6), F32),
        compiler_params=_cparams(("parallel", "parallel")),
    )(c_all, w_ada, b_ada.reshape(L, 1, D6))


def _mod_spec(j, bb, D):
    return pl.BlockSpec((None, bb, 1, D), lambda b, t, j=j: (j, b, 0, 0))


_IN_AG = (0, 512)
_IN_QKV = (512, 2048)
_IN_Z = (2048, 2560)
_IN_LRU = (2560, 3072)
_IN_BA = (3072, 3200)
_IN_COLS = 3200


def _inproj_kernel(x_ref, sc_ref, sh_ref, g_ref, w_ref, u_ref, qkv_ref, z_ref, lru_ref, ba_ref):
    bb, tt, D = x_ref.shape
    x = x_ref[...]
    h = x * lax.rsqrt(jnp.mean(x * x, axis=-1, keepdims=True) + EPS) * g_ref[...]
    h = h * (1.0 + sc_ref[...]) + sh_ref[...]
    hb = h.reshape(bb * tt, D).astype(BF16)

    def seg(lo_hi):
        return jnp.dot(hb, w_ref[:, lo_hi[0]:lo_hi[1]], preferred_element_type=F32)

    ag = seg(_IN_AG)
    u_ref[...] = (ag[:, :256] * _sigmoid(ag[:, 256:])).reshape(bb, tt, 256)
    qkv_ref[...] = seg(_IN_QKV).reshape(bb, tt, 1536)
    z_ref[...] = seg(_IN_Z).reshape(bb, tt, 512)
    lru_ref[...] = seg(_IN_LRU).reshape(bb, tt, 512)
    ba_ref[...] = seg(_IN_BA).reshape(bb, tt, LANES)


def _inproj(x, mod, gain, w):
    B, T, D = x.shape
    bb, tt = _tok_tiles(B, T)
    widths = (256, 1536, 512, 512, LANES)
    return pl.pallas_call(
        _inproj_kernel, grid=(B // bb, T // tt),
        in_specs=[pl.BlockSpec((bb, tt, D), lambda b, t: (b, t, 0)),
                  _mod_spec(1, bb, D), _mod_spec(0, bb, D),
                  pl.BlockSpec((1, D), lambda b, t: (0, 0)),
                  pl.BlockSpec((D, _IN_COLS), lambda b, t: (0, 0))],
        out_specs=[pl.BlockSpec((bb, tt, n), lambda b, t: (b, t, 0)) for n in widths],
        out_shape=[jax.ShapeDtypeStruct((B, T, n), F32) for n in widths],
        compiler_params=_cparams(("parallel", "parallel")),
    )(x, mod, mod, gain, w)


def _mixa_kernel(u_ref, buf_ref, w_ref, cb_ref, lg_ref, lb_ref, y_ref, ns_ref, xp_ref):
    bb, tc, ch = u_ref.shape
    width = w_ref.shape[0]
    t = pl.program_id(1)

    @pl.when(t == 0)
    def _():
        xp_ref[:, 2:32, :] = buf_ref[...]

    @pl.when(t > 0)
    def _():
        xp_ref[:, 0:32, :] = xp_ref[:, tc:tc + 32, :]

    xp_ref[:, 32:32 + tc, :] = u_ref[...]
    ns_ref[...] = xp_ref[:, tc + 2:tc + 32, :]

    rs = min(tc, 64)
    for r0 in range(0, tc, rs):
        acc = jnp.zeros((bb, rs, ch), F32) + cb_ref[...]
        for j in range(width):
            acc = acc + w_ref[j:j + 1, :] * xp_ref[:, r0 + 2 + j:r0 + 2 + j + rs, :]
        mu = jnp.mean(acc, axis=-1, keepdims=True)
        d = acc - mu
        var = jnp.mean(d * d, axis=-1, keepdims=True)
        yn = d * lax.rsqrt(var + EPS) * lg_ref[...] + lb_ref[...]
        y_ref[:, r0:r0 + rs, :] = _silu(yn)


def _mixa(u, buf, w, cb, lg, lb):
    B, T, ch = u.shape
    if T >= 256:
        bb, tc = 1, 256
    else:
        bb, tc = min(B, 8), T
    assert T % tc == 0 and B % bb == 0
    nb = buf.shape[1]
    vec = pl.BlockSpec((1, ch), lambda b, t: (0, 0))
    return pl.pallas_call(
        _mixa_kernel, grid=(B // bb, T // tc),
        in_specs=[pl.BlockSpec((bb, tc, ch), lambda b, t: (b, t, 0)),
                  pl.BlockSpec((bb, nb, ch), lambda b, t: (b, 0, 0)),
                  pl.BlockSpec(w.shape, lambda b, t: (0, 0)), vec, vec, vec],
        out_specs=[pl.BlockSpec((bb, tc, ch), lambda b, t: (b, t, 0)),
                   pl.BlockSpec((bb, nb, ch), lambda b, t: (b, 0, 0))],
        out_shape=[jax.ShapeDtypeStruct((B, T, ch), F32), jax.ShapeDtypeStruct((B, nb, ch), F32)],
        scratch_shapes=[pltpu.VMEM((bb, 32 + tc, ch), F32)],
        compiler_params=_cparams(("parallel", "arbitrary")),
    )(u, buf, w, cb, lg, lb)


def _gdn_kernel(qkv_ref, z_ref, ba_ref, sq_ref, sd_ref, cw_ref, alog_ref, dtb_ref, gn_ref,
                y_ref, nq_ref, nd_ref, xp_ref, qn_ref, bg_ref, s_ref, *, C, hp_solve):
    bb, tt, width = qkv_ref.shape
    nh = sd_ref.shape[1]
    dk = sd_ref.shape[2]
    ntap = cw_ref.shape[0]
    t = pl.program_id(1)

    @pl.when(t == 0)
    def _():
        xp_ref[:, 5:8, :] = sq_ref[...]
        s_ref[...] = sd_ref[...]

    @pl.when(t > 0)
    def _():
        xp_ref[:, 5:8, :] = xp_ref[:, tt + 5:tt + 8, :]

    xp_ref[:, 8:8 + tt, :] = qkv_ref[...]
    nq_ref[...] = xp_ref[:, tt + 5:tt + 8, :]

    rs = min(tt, 128)
    for sl in range(width // dk):
        lo = sl * dk
        for r0 in range(0, tt, rs):
            acc = jnp.zeros((bb, rs, dk), F32)
            for j in range(ntap):
                acc = acc + cw_ref[j:j + 1, lo:lo + dk] * xp_ref[:, r0 + 5 + j:r0 + 5 + j + rs, lo:lo + dk]
            s = _silu(acc)
            if sl < 2 * nh:
                s = s * lax.rsqrt(jnp.sum(s * s, axis=-1, keepdims=True) + EPS)
            if sl < nh:
                s = s * (dk ** -0.5)
            qn_ref[:, r0:r0 + rs, lo:lo + dk] = s

    ba = ba_ref[...]
    lane3 = lax.broadcasted_iota(jnp.int32, ba.shape, 2)
    gdec = -jnp.exp(alog_ref[...]) * _softplus(ba + dtb_ref[...])
    bg_ref[...] = jnp.where(lane3 < nh, _sigmoid(ba), jnp.where(lane3 < 2 * nh, gdec, 0.0))

    ii = lax.broadcasted_iota(jnp.int32, (C, C), 0)
    jj = lax.broadcasted_iota(jnp.int32, (C, C), 1)
    tri_incl = jnp.where(ii >= jj, 1.0, 0.0).astype(BF16)
    lane_c = lax.broadcasted_iota(jnp.int32, (C, LANES), 1)
    nsteps = C.bit_length() - 1
    assert (1 << nsteps) == C
    ident = jnp.where(ii == jj, 1.0, 0.0)
    pair_mask = []
    for s in range(nsteps):
        same_big = jnp.right_shift(ii, s + 1) == jnp.right_shift(jj, s + 1)
        diff_small = jnp.right_shift(ii, s) != jnp.right_shift(jj, s)
        pair_mask.append((same_big, diff_small))

    def pair_part(s, m):
        return jnp.where(pair_mask[s][0], jnp.where(pair_mask[s][1], m, 0.0), 0.0)

    def psolve(m, x):
        return _dot_hp(m, x) if hp_solve else _dot(m, x)

    def chunk(c, carry):
        r0 = pl.multiple_of(c * C, C)
        for b in range(bb):
            bgc = bg_ref[b, pl.ds(r0, C), :]
            gcs = _dot_exact_lhs(tri_incl, jnp.where(lane_c < nh, 0.0, bgc))
            gh, gm, glo = _split3(gcs)
            nt = functools.partial(lax.dot_general, dimension_numbers=(((1,), (1,)), ((), ())),
                                   preferred_element_type=F32)
            for h in range(nh):
                q = qn_ref[b, pl.ds(r0, C), h * dk:(h + 1) * dk]
                k = qn_ref[b, pl.ds(r0, C), (nh + h) * dk:(nh + h + 1) * dk]
                v = qn_ref[b, pl.ds(r0, C), (2 * nh + h) * dk:(2 * nh + h + 1) * dk]
                beta = bgc[:, h:h + 1]
                gcol = gcs[:, nh + h:nh + h + 1]
                gl = gcs[C - 1:C, nh + h:nh + h + 1]
                sel = jnp.where(lane_c == nh + h, 1.0, 0.0).astype(BF16)
                grow = nt(sel, gh) + (nt(sel, gm) + nt(sel, glo))
                decay = jnp.exp(jnp.where(ii >= jj, gcol - grow, -jnp.inf))
                kb = k * beta
                a2 = _dot_nt(jnp.concatenate([kb, q], axis=0), k)
                low = jnp.where(ii > jj, a2[:C] * decay, 0.0)
                qk = a2[C:] * decay
                rhs = jnp.concatenate([v * beta, kb * jnp.exp(gcol)], axis=1)
                tinv = ident - pair_part(0, low)
                for lvl in range(1, nsteps):
                    tinv = tinv - psolve(psolve(tinv, pair_part(lvl, low)), tinv)
                x = psolve(tinv, rhs)
                u = x[:, :dk]
                w = x[:, dk:]
                st = s_ref[b, h]
                wq = _dot(jnp.concatenate([w, q * jnp.exp(gcol)], axis=0), st)
                v_new = u - wq[:C]
                o = wq[C:] + _dot(qk, v_new)
                kdec = k * jnp.exp(gl - gcol)
                upd = lax.dot_general(kdec.astype(BF16), v_new.astype(BF16), (((0,), (0,)), ((), ())),
                                      preferred_element_type=F32)
                s_ref[b, h] = st * jnp.exp(gl) + upd
                o = o * lax.rsqrt(jnp.mean(o * o, axis=-1, keepdims=True) + EPS) * gn_ref[...]
                zz = z_ref[b, pl.ds(r0, C), h * dk:(h + 1) * dk]
                y_ref[b, pl.ds(r0, C), h * dk:(h + 1) * dk] = o * _silu(zz)
        return carry

    lax.fori_loop(0, tt // C, chunk, 0)
    nd_ref[...] = s_ref[...]


def _gdn(qkv, z, ba, s_qkv, s_delta, cw, alog_vec, dtb_vec, gn, hp_solve=True):
    B, T, width = qkv.shape
    nh, dk = s_delta.shape[1], s_delta.shape[2]
    C = min(GDN_CHUNK, T)
    if T >= 512:
        bb, tt = 1, 512
    elif T > C:
        bb, tt = 1, T
    else:
        bb, tt = min(B, 2), T
    assert T % tt == 0 and tt % C == 0 and B % bb == 0
    vec = pl.BlockSpec((1, LANES), lambda b, t: (0, 0))
    kern = functools.partial(_gdn_kernel, C=C, hp_solve=hp_solve)
    return pl.pallas_call(
        kern, grid=(B // bb, T // tt),
        in_specs=[pl.BlockSpec((bb, tt, width), lambda b, t: (b, t, 0)),
                  pl.BlockSpec((bb, tt, nh * dk), lambda b, t: (b, t, 0)),
                  pl.BlockSpec((bb, tt, LANES), lambda b, t: (b, t, 0)),
                  pl.BlockSpec((bb, 3, width), lambda b, t: (b, 0, 0)),
                  pl.BlockSpec((bb, nh, dk, dk), lambda b, t: (b, 0, 0, 0)),
                  pl.BlockSpec(cw.shape, lambda b, t: (0, 0)), vec, vec, vec],
        out_specs=[pl.BlockSpec((bb, tt, nh * dk), lambda b, t: (b, t, 0)),
                   pl.BlockSpec((bb, 3, width), lambda b, t: (b, 0, 0)),
                   pl.BlockSpec((bb, nh, dk, dk), lambda b, t: (b, 0, 0, 0))],
        out_shape=[jax.ShapeDtypeStruct((B, T, nh * dk), F32),
                   jax.ShapeDtypeStruct((B, 3, width), F32),
                   jax.ShapeDtypeStruct((B, nh, dk, dk), F32)],
        scratch_shapes=[pltpu.VMEM((bb, 8 + tt, width), F32), pltpu.VMEM((bb, tt, width), F32),
                        pltpu.VMEM((bb, tt, LANES), F32), pltpu.VMEM((bb, nh, dk, dk), F32)],
        compiler_params=_cparams(("parallel", "arbitrary")),
    )(qkv, z, ba, s_qkv, s_delta, cw, alog_vec, dtb_vec, gn)


def _lru_kernel(lru_ref, cs_ref, h0_ref, cw_ref, cb_ref, wa_ref, ba_ref, wi_ref, bi_ref, lam_ref, ng_ref,
                y_ref, ncs_ref, nh_ref, xp_ref, sa_ref, sb_ref, h_ref):
    bb, tc, two_ch = lru_ref.shape
    ch = two_ch // 2
    ntap = cw_ref.shape[0]
    t = pl.program_id(1)

    @pl.when(t == 0)
    def _():
        xp_ref[:, 5:8, :] = cs_ref[...]
        h_ref[...] = h0_ref[...]

    @pl.when(t > 0)
    def _():
        xp_ref[:, 5:8, :] = xp_ref[:, tc + 5:tc + 8, :]

    xp_ref[:, 8:8 + tc, :] = lru_ref[:, :, 0:ch]
    ncs_ref[...] = xp_ref[:, tc + 5:tc + 8, :]

    xc = jnp.zeros((bb, tc, ch), F32) + cb_ref[...]
    for j in range(ntap):
        xc = xc + cw_ref[j:j + 1, :] * xp_ref[:, 5 + j:5 + j + tc, :]
    x2 = xc.reshape(bb * tc, ch)
    r = _sigmoid(_dot_hp(x2, wa_ref[...]) + ba_ref[...])
    i = _sigmoid(_dot_hp(x2, wi_ref[...]) + bi_ref[...])
    log_a = (-LRU_C * r) * _softplus(-lam_ref[...])
    a = jnp.exp(log_a)
    bv = jnp.sqrt(1.0 - jnp.exp(2.0 * log_a)) * (i * x2)
    a3 = a.reshape(bb, tc, ch)
    b3 = bv.reshape(bb, tc, ch)

    sa_ref[:, 0:tc, :] = jnp.ones((bb, tc, ch), F32)
    sb_ref[:, 0:tc, :] = jnp.zeros((bb, tc, ch), F32)
    d = 1
    while d < tc:
        sa_ref[:, tc:2 * tc, :] = a3
        sb_ref[:, tc:2 * tc, :] = b3
        ap = sa_ref[:, tc - d:2 * tc - d, :]
        bp = sb_ref[:, tc - d:2 * tc - d, :]
        b3 = a3 * bp + b3
        a3 = a3 * ap
        d *= 2
    h = a3 * h_ref[...] + b3
    h_last = h[:, tc - 1:tc, :]
    h_ref[...] = h_last
    nh_ref[...] = h_last

    y = h * jax.nn.gelu(lru_ref[:, :, ch:two_ch])
    y_ref[...] = y * lax.rsqrt(jnp.mean(y * y, axis=-1, keepdims=True) + EPS) * ng_ref[...]


def _lru(lru, cs, h0, cw, cb, wa, ba, wi, bi, lam, ng):
    B, T, two_ch = lru.shape
    ch = two_ch // 2
    if T >= 256:
        bb, tc = 1, 256
    else:
        bb, tc = min(B, 16), T
    assert T % tc == 0 and B % bb == 0
    vec = pl.BlockSpec((1, ch), lambda b, t: (0, 0))
    mat = pl.BlockSpec((ch, ch), lambda b, t: (0, 0))
    return pl.pallas_call(
        _lru_kernel, grid=(B // bb, T // tc),
        in_specs=[pl.BlockSpec((bb, tc, two_ch), lambda b, t: (b, t, 0)),
                  pl.BlockSpec((bb, 3, ch), lambda b, t: (b, 0, 0)),
                  pl.BlockSpec((bb, 1, ch), lambda b, t: (b, 0, 0)),
                  pl.BlockSpec(cw.shape, lambda b, t: (0, 0)), vec, mat, vec, mat, vec, vec, vec],
        out_specs=[pl.BlockSpec((bb, tc, ch), lambda b, t: (b, t, 0)),
                   pl.BlockSpec((bb, 3, ch), lambda b, t: (b, 0, 0)),
                   pl.BlockSpec((bb, 1, ch), lambda b, t: (b, 0, 0))],
        out_shape=[jax.ShapeDtypeStruct((B, T, ch), F32), jax.ShapeDtypeStruct((B, 3, ch), F32),
                   jax.ShapeDtypeStruct((B, 1, ch), F32)],
        scratch_shapes=[pltpu.VMEM((bb, 8 + tc, ch), F32), pltpu.VMEM((bb, 2 * tc, ch), F32),
                        pltpu.VMEM((bb, 2 * tc, ch), F32), pltpu.VMEM((bb, 1, ch), F32)],
        compiler_params=_cparams(("parallel", "arbitrary")),
    )(lru, cs, h0, cw, cb, wa, ba, wi, bi, lam, ng)


def _outproj_kernel(ya_ref, yb_ref, yc_ref, x_ref, g1_ref, sc_ref, sh_ref, n2_ref, w_ref, x1_ref, h2_ref):
    bb, tt, D = x_ref.shape
    m = bb * tt
    na, nb = ya_ref.shape[2], yb_ref.shape[2]
    mix = (_dot(ya_ref[...].reshape(m, na), w_ref[0:na, :])
           + _dot(yb_ref[...].reshape(m, nb), w_ref[na:na + nb, :])
           + _dot(yc_ref[...].reshape(m, D - na - nb), w_ref[na + nb:D, :]))
    x1 = x_ref[...] + g1_ref[...] * mix.reshape(bb, tt, D)
    x1_ref[...] = x1
    h = x1 * lax.rsqrt(jnp.mean(x1 * x1, axis=-1, keepdims=True) + EPS) * n2_ref[...]
    h = h * (1.0 + sc_ref[...]) + sh_ref[...]
    h2_ref[...] = h.reshape(m, D)


def _outproj(ya, yb, yc, x, mod, gain2, w):
    B, T, D = x.shape
    bb, tt = _tok_tiles(B, T)
    nt = T // tt

    def tok(n):
        return pl.BlockSpec((bb, tt, n), lambda b, t: (b, t, 0))

    return pl.pallas_call(
        _outproj_kernel, grid=(B // bb, nt),
        in_specs=[tok(ya.shape[2]), tok(yb.shape[2]), tok(yc.shape[2]), tok(D),
                  _mod_spec(2, bb, D), _mod_spec(4, bb, D), _mod_spec(3, bb, D),
                  pl.BlockSpec((1, D), lambda b, t: (0, 0)),
                  pl.BlockSpec((D, D), lambda b, t: (0, 0))],
        out_specs=[tok(D), pl.BlockSpec((bb * tt, D), lambda b, t: (b * nt + t, 0))],
        out_shape=[jax.ShapeDtypeStruct((B, T, D), F32), jax.ShapeDtypeStruct((B * T, D), F32)],
        compiler_params=_cparams(("parallel", "parallel")),
    )(ya, yb, yc, x, mod, mod, mod, gain2, w)


def _ffn_kernel(h_ref, x1_ref, g2_ref, wg_ref, wu_ref, wd_ref, o_ref, acc_ref):
    bb, tt, D = x1_ref.shape
    j = pl.program_id(2)

    @pl.when(j == 0)
    def _():
        acc_ref[...] = jnp.zeros_like(acc_ref)

    hb = h_ref[...].astype(BF16)
    g = jnp.dot(hb, wg_ref[...], preferred_element_type=F32)
    u = jnp.dot(hb, wu_ref[...], preferred_element_type=F32)
    acc_ref[...] += _dot(_silu(g) * u, wd_ref[...])

    @pl.when(j == pl.num_programs(2) - 1)
    def _():
        o_ref[...] = x1_ref[...] + g2_ref[...] * acc_ref[...].reshape(bb, tt, D)


def _ffn(h2, x1, mod, wg, wu, wd):
    B, T, D = x1.shape
    bb, tt = _tok_tiles(B, T)
    nt = T // tt
    F = wg.shape[1]
    tf = F // 2
    assert tf % LANES == 0
    return pl.pallas_call(
        _ffn_kernel, grid=(B // bb, nt, F // tf),
        in_specs=[pl.BlockSpec((bb * tt, D), lambda b, t, j: (b * nt + t, 0)),
                  pl.BlockSpec((bb, tt, D), lambda b, t, j: (b, t, 0)),
                  pl.BlockSpec((None, bb, 1, D), lambda b, t, j: (5, b, 0, 0)),
                  pl.BlockSpec((D, tf), lambda b, t, j: (0, j)),
                  pl.BlockSpec((D, tf), lambda b, t, j: (0, j)),
                  pl.BlockSpec((tf, D), lambda b, t, j: (j, 0))],
        out_specs=pl.BlockSpec((bb, tt, D), lambda b, t, j: (b, t, 0)),
        out_shape=jax.ShapeDtypeStruct((B, T, D), F32),
        scratch_shapes=[pltpu.VMEM((bb * tt, D), F32)],
        compiler_params=_cparams(("parallel", "parallel", "arbitrary")),
    )(h2, x1, mod, wg, wu, wd)


def _router_kernel(h_ref, wr_ref, br_ref, idx_ref, gate_ref, cnt_ref, cnt_acc):
    i = pl.program_id(0)

    @pl.when(i == 0)
    def _():
        cnt_acc[...] = jnp.zeros_like(cnt_acc)

    logits = _dot_hp(h_ref[...], wr_ref[...]) + br_ref[...]
    lane = lax.broadcasted_iota(jnp.int32, logits.shape, 1).astype(F32)
    m1 = jnp.max(logits, axis=-1, keepdims=True)
    i1 = jnp.min(jnp.where(logits == m1, lane, float(LANES)), axis=-1, keepdims=True)
    rest = jnp.where(lane == i1, NEG_BIG, logits)
    m2 = jnp.max(rest, axis=-1, keepdims=True)
    i2 = jnp.min(jnp.where(rest == m2, lane, float(LANES)), axis=-1, keepdims=True)
    e = jnp.exp(m2 - m1)
    g1 = 1.0 / (1.0 + e)
    g2 = e / (1.0 + e)
    idx_ref[...] = jnp.where(lane == 0.0, i1, jnp.where(lane == 1.0, i2, 0.0)).astype(jnp.int32)
    gate_ref[...] = jnp.where(lane == 0.0, g1, jnp.where(lane == 1.0, g2, 0.0))
    oh = jnp.where(lane == i1, 1.0, jnp.where(lane == i2, 1.0, 0.0))
    cnt_acc[...] += jnp.sum(oh, axis=0, keepdims=True)
    cnt_ref[...] = jnp.broadcast_to(cnt_acc[...], cnt_ref.shape)


def _router(h2, wr, br, tm):
    N, D = h2.shape
    return pl.pallas_call(
        _router_kernel, grid=(N // tm,),
        in_specs=[pl.BlockSpec((tm, D), lambda i: (i, 0)),
                  pl.BlockSpec((D, LANES), lambda i: (0, 0)),
                  pl.BlockSpec((1, LANES), lambda i: (0, 0))],
        out_specs=[pl.BlockSpec((tm, LANES), lambda i: (i, 0)),
                   pl.BlockSpec((tm, LANES), lambda i: (i, 0)),
                   pl.BlockSpec((8, LANES), lambda i: (0, 0))],
        out_shape=[jax.ShapeDtypeStruct((N, LANES), jnp.int32), jax.ShapeDtypeStruct((N, LANES), F32),
                   jax.ShapeDtypeStruct((8, LANES), F32)],
        scratch_shapes=[pltpu.VMEM((1, LANES), F32)],
        compiler_params=_cparams(("arbitrary",)),
    )(h2, wr, br)


def _slot_kernel(idx_ref, pstart_ref, dest_ref, carry):
    i = pl.program_id(0)

    @pl.when(i == 0)
    def _():
        carry[...] = jnp.zeros_like(carry)

    tm = idx_ref.shape[0]
    idx = idx_ref[...]
    lane = lax.broadcasted_iota(jnp.int32, idx.shape, 1)
    oh1 = lane == idx[:, 0:1]
    oh2 = lane == idx[:, 1:2]
    oh = jnp.where(oh1, 1.0, jnp.where(oh2, 1.0, 0.0))
    ii = lax.broadcasted_iota(jnp.int32, (tm, tm), 0)
    jj = lax.broadcasted_iota(jnp.int32, (tm, tm), 1)
    earlier = jnp.dot(jnp.where(ii > jj, 1.0, 0.0).astype(BF16), oh.astype(BF16), preferred_element_type=F32)
    base = earlier + (carry[...] + pstart_ref[...])
    d1 = jnp.sum(jnp.where(oh1, base, 0.0), axis=-1, keepdims=True)
    d2 = jnp.sum(jnp.where(oh2, base, 0.0), axis=-1, keepdims=True)
    dest_ref[...] = jnp.where(lane == 0, d1, jnp.where(lane == 1, d2, 0.0)).astype(jnp.int32)
    carry[...] += jnp.sum(oh, axis=0, keepdims=True)


def _slots(idx, pstart, tm):
    N = idx.shape[0]
    return pl.pallas_call(
        _slot_kernel, grid=(N // tm,),
        in_specs=[pl.BlockSpec((tm, LANES), lambda i: (i, 0)), pl.BlockSpec((1, LANES), lambda i: (0, 0))],
        out_specs=pl.BlockSpec((tm, LANES), lambda i: (i, 0)),
        out_shape=jax.ShapeDtypeStruct((N, LANES), jnp.int32),
        scratch_shapes=[pltpu.VMEM((1, LANES), F32)],
        compiler_params=_cparams(("arbitrary",)),
    )(idx, pstart)


def _dispatch_kernel(dest_hbm, h_ref, xs_in, xs_ref, dsm, isem, sem):
    del xs_in
    tm = h_ref.shape[0]
    i = pl.program_id(0)
    cp = pltpu.make_async_copy(dest_hbm.at[i], dsm, isem)
    cp.start()
    cp.wait()

    def row_copy(r, d):
        return pltpu.make_async_copy(h_ref.at[pl.ds(r, 1)], xs_ref.at[pl.ds(d, 1)], sem)

    def issue(r, c):
        row_copy(r, dsm[r]).start()
        row_copy(r, dsm[tm + r]).start()
        return c

    lax.fori_loop(0, tm, issue, 0)

    def drain(r, c):
        row_copy(0, 0).wait()
        row_copy(0, 0).wait()
        return c

    lax.fori_loop(0, tm, drain, 0)


def _dispatch(dest_tiles, h2, rows, tm):
    N, D = h2.shape
    return pl.pallas_call(
        _dispatch_kernel, grid=(N // tm,),
        in_specs=[pl.BlockSpec(memory_space=pl.ANY),
                  pl.BlockSpec((tm, D), lambda i: (i, 0)),
                  pl.BlockSpec(memory_space=pl.ANY)],
        out_specs=pl.BlockSpec(memory_space=pl.ANY),
        out_shape=jax.ShapeDtypeStruct((rows, D), F32),
        scratch_shapes=[pltpu.SMEM((2 * tm,), jnp.int32), pltpu.SemaphoreType.DMA, pltpu.SemaphoreType.DMA],
        input_output_aliases={2: 0},
        compiler_params=_cparams(("arbitrary",)),
    )(dest_tiles, h2, jnp.zeros((rows, D), F32))


def _expert_kernel(be_ref, nu_ref, xs_ref, wg_ref, wu_ref, wd_ref, ys_ref):
    del be_ref
    used = pl.program_id(0) < nu_ref[0]

    @pl.when(used)
    def _():
        xb = xs_ref[...].astype(BF16)
        g = jnp.dot(xb, wg_ref[...], preferred_element_type=F32)
        u = jnp.dot(xb, wu_ref[...], preferred_element_type=F32)
        ys_ref[...] = _dot(_silu(g) * u, wd_ref[...])

    @pl.when(jnp.logical_not(used))
    def _():
        ys_ref[...] = jnp.zeros_like(ys_ref)


def _experts(block_e, n_used, xs, wg, wu, wd):
    rows, D = xs.shape
    F = wg.shape[2]
    nblk = rows // MOE_ROWS

    def row_map(i, be, nu):
        return (jnp.minimum(i, nu[0] - 1), 0)

    def w_map(i, be, nu):
        return (be[i], 0, 0)

    return pl.pallas_call(
        _expert_kernel,
        grid_spec=pltpu.PrefetchScalarGridSpec(
            num_scalar_prefetch=2, grid=(nblk,),
            in_specs=[pl.BlockSpec((MOE_ROWS, D), row_map),
                      pl.BlockSpec((None, D, F), w_map), pl.BlockSpec((None, D, F), w_map),
                      pl.BlockSpec((None, F, D), w_map)],
            out_specs=pl.BlockSpec((MOE_ROWS, D), lambda i, be, nu: (i, 0))),
        out_shape=jax.ShapeDtypeStruct((rows, D), F32),
        compiler_params=_cparams(("arbitrary",)),
    )(block_e, n_used, xs, wg, wu, wd)


def _combine_kernel(dest_hbm, ys_hbm, gate_ref, x1_ref, g2_ref, o_ref, dsm, b0, b1, isem, sem):
    bb, tt, D = x1_ref.shape
    tm = bb * tt
    i = pl.program_id(0) * pl.num_programs(1) + pl.program_id(1)
    cp = pltpu.make_async_copy(dest_hbm.at[i], dsm, isem)
    cp.start()
    cp.wait()

    def row_copy(d, buf, r):
        return pltpu.make_async_copy(ys_hbm.at[pl.ds(d, 1)], buf.at[pl.ds(r, 1)], sem)

    def issue(r, c):
        row_copy(dsm[r], b0, r).start()
        row_copy(dsm[tm + r], b1, r).start()
        return c

    lax.fori_loop(0, tm, issue, 0)

    def drain(r, c):
        row_copy(0, b0, 0).wait()
        row_copy(0, b1, 0).wait()
        return c

    lax.fori_loop(0, tm, drain, 0)
    g = gate_ref[...]
    y = g[:, 0:1] * b0[...] + g[:, 1:2] * b1[...]
    o_ref[...] = x1_ref[...] + g2_ref[...] * y.reshape(bb, tt, D)


def _combine(dest_tiles, ys, gates, x1, mod):
    B, T, D = x1.shape
    bb, tt = _tok_tiles(B, T)
    nt = T // tt
    tm = bb * tt
    return pl.pallas_call(
        _combine_kernel, grid=(B // bb, nt),
        in_specs=[pl.BlockSpec(memory_space=pl.ANY), pl.BlockSpec(memory_space=pl.ANY),
                  pl.BlockSpec((tm, LANES), lambda b, t: (b * nt + t, 0)),
                  pl.BlockSpec((bb, tt, D), lambda b, t: (b, t, 0)),
                  _mod_spec(5, bb, D)],
        out_specs=pl.BlockSpec((bb, tt, D), lambda b, t: (b, t, 0)),
        out_shape=jax.ShapeDtypeStruct((B, T, D), F32),
        scratch_shapes=[pltpu.SMEM((2 * tm,), jnp.int32), pltpu.VMEM((tm, D), F32), pltpu.VMEM((tm, D), F32),
                        pltpu.SemaphoreType.DMA, pltpu.SemaphoreType.DMA],
        compiler_params=_cparams(("arbitrary", "arbitrary")),
    )(dest_tiles, ys, gates, x1, mod)


def _moe(h2, x1, mod, wr, br, wg, wu, wd):
    B, T, D = x1.shape
    bb, tt = _tok_tiles(B, T)
    tm = bb * tt
    N = B * T
    idx, gates, cnt = _router(h2, wr, br, tm)
    counts = cnt[0, :N_EXPERTS].astype(jnp.int32)
    padded = (counts + MOE_ROWS - 1) // MOE_ROWS * MOE_ROWS
    pends = jnp.cumsum(padded)
    pstart = pends - padded
    nblk = -(-(2 * N) // MOE_ROWS) + N_EXPERTS
    block_e = jnp.minimum(jnp.searchsorted(pends, jnp.arange(nblk, dtype=jnp.int32) * MOE_ROWS, side='right'),
                          N_EXPERTS - 1).astype(jnp.int32)
    n_used = (pends[-1:] // MOE_ROWS).astype(jnp.int32)
    pstart_vec = jnp.zeros((1, LANES), F32).at[0, :N_EXPERTS].set(pstart.astype(F32))
    dest = _slots(idx, pstart_vec, tm)
    dest_tiles = dest[:, :2].reshape(N // tm, tm, 2).transpose(0, 2, 1).reshape(N // tm, 2 * tm)
    xs = _dispatch(dest_tiles, h2, nblk * MOE_ROWS, tm)
    ys = _experts(block_e, n_used, xs, wg, wu, wd)
    return _combine(dest_tiles, ys, gates, x1, mod)


def _final_kernel(x_ref, g_ref, o_ref):
    x = x_ref[...]
    o_ref[...] = x * lax.rsqrt(jnp.mean(x * x, axis=-1, keepdims=True) + EPS) * g_ref[...]


def _final(x, g):
    B, T, D = x.shape
    bb, tt = _tok_tiles(B, T)
    return pl.pallas_call(
        _final_kernel, grid=(B // bb, T // tt),
        in_specs=[pl.BlockSpec((bb, tt, D), lambda b, t: (b, t, 0)), pl.BlockSpec((1, D), lambda b, t: (0, 0))],
        out_specs=pl.BlockSpec((bb, tt, D), lambda b, t: (b, t, 0)),
        out_shape=jax.ShapeDtypeStruct((B, T, D), F32),
        compiler_params=_cparams(("parallel", "parallel")),
    )(x, g)


def _trunk(x, mods, st, p):
    L = mods.shape[0]
    new = [[] for _ in range(5)]
    for l in range(L):
        mod = mods[l]
        u, qkv, z, lru, ba = _inproj(x, mod, p['norm1'][l], p['w_in'][l])
        ya, n_a = _mixa(u, st[0][l], p['conv_a_w'][l], p['conv_a_b'][l], p['ln_a_g'][l], p['ln_a_b'][l])
        yb, n_q, n_d = _gdn(qkv, z, ba, st[1][l], st[2][l], p['conv_qkv_w'][l], p['alog'][l], p['dtb'][l],
                            p['gdn_norm'][l])
        yc, n_cl, n_h = _lru(lru, st[3][l], st[4][l][:, None, :], p['conv_lru_w'][l], p['conv_lru_b'][l],
                             p['lru_wa'][l], p['lru_b_a'][l], p['lru_wi'][l], p['lru_b_i'][l],
                             p['lru_lambda'][l], p['lru_norm'][l])
        x1, h2 = _outproj(ya, yb, yc, x, mod, p['norm2'][l], p['w_out'][l])
        j = l // 2
        if l % 2 == 0:
            x = _ffn(h2, x1, mod, p['ffn_wg'][j], p['ffn_wu'][j], p['ffn_wd'][j])
        else:
            x = _moe(h2, x1, mod, p['router_w'][j], p['router_b'][j], p['moe_wg'][j], p['moe_wu'][j],
                     p['moe_wd'][j])
        for lst, val in zip(new, (n_a, n_q, n_d, n_cl, n_h[:, 0, :])):
            lst.append(val)
    y = _final(x, p['final_norm'])
    return y, [jnp.stack(v, axis=0) for v in new]


def kernel(x_prompt, x_sample, c_prompt, c_sample, state_conv_a, state_conv_qkv, state_delta, state_conv_lru, state_lru, w_ada, b_ada, norm1, norm2, w_in, w_out, conv_a_w, conv_a_b, ln_a_g, ln_a_b, conv_qkv_w, gdn_a_log, gdn_dt_bias, gdn_norm, conv_lru_w, conv_lru_b, lru_w_a, lru_b_a, lru_w_i, lru_b_i, lru_lambda, lru_norm, ffn_wg, ffn_wu, ffn_wd, router_w, router_b, moe_wg, moe_wu, moe_wd, final_norm):
    L, D = norm1.shape
    Bp, Bs = x_prompt.shape[0], x_sample.shape[0]
    nh = gdn_a_log.shape[1]

    def row(v):
        return v[:, None, :]

    def lanes(v):
        return jnp.zeros((L, 1, LANES), F32).at[:, 0, nh:2 * nh].set(v)

    def block_diag(w):
        n, k = w.shape[1], w.shape[2]
        eye = jnp.eye(n, dtype=w.dtype)
        return jnp.einsum('lnij,nm->lnimj', w, eye).reshape(L, n * k, n * k)

    split = 2 * 256 + 3 * 512 + 512
    w_in_r = jnp.concatenate([w_in[:, :, :split], w_in[:, :, split + 2 * nh:], w_in[:, :, split:split + 2 * nh],
                              jnp.zeros((L, D, LANES - 2 * nh), w_in.dtype)], axis=-1).astype(BF16)
    n_moe = router_w.shape[0]
    p = dict(
        norm1=row(norm1), norm2=row(norm2), w_in=w_in_r, w_out=w_out.astype(BF16),
        conv_a_w=conv_a_w, conv_a_b=row(conv_a_b), ln_a_g=row(ln_a_g), ln_a_b=row(ln_a_b),
        conv_qkv_w=conv_qkv_w, alog=lanes(gdn_a_log), dtb=lanes(gdn_dt_bias), gdn_norm=row(gdn_norm),
        conv_lru_w=conv_lru_w, conv_lru_b=row(conv_lru_b), lru_wa=block_diag(lru_w_a), lru_b_a=row(lru_b_a),
        lru_wi=block_diag(lru_w_i), lru_b_i=row(lru_b_i), lru_lambda=row(lru_lambda), lru_norm=row(lru_norm),
        ffn_wg=ffn_wg.astype(BF16), ffn_wu=ffn_wu.astype(BF16), ffn_wd=ffn_wd.astype(BF16),
        router_w=jnp.concatenate([router_w, jnp.zeros((n_moe, D, LANES - N_EXPERTS), F32)], axis=-1),
        router_b=jnp.concatenate([router_b, jnp.full((n_moe, LANES - N_EXPERTS), NEG_BIG, F32)],
                                 axis=-1)[:, None, :],
        moe_wg=moe_wg.astype(BF16), moe_wu=moe_wu.astype(BF16), moe_wd=moe_wd.astype(BF16),
        final_norm=final_norm[None, :],
    )

    c_all = jnp.concatenate([c_prompt, c_sample], axis=0)
    mod = _ada(c_all, w_ada, b_ada)
    mod = mod.reshape(L, Bp + Bs, 6, D).transpose(0, 2, 1, 3)[:, :, :, None, :]
    mod_p, mod_s = mod[:, :, :Bp], mod[:, :, Bp:]

    zero_states = [jnp.zeros((L, Bp) + s.shape[2:], F32)
                   for s in (state_conv_a, state_conv_qkv, state_delta, state_conv_lru, state_lru)]
    y_p, st_p = _trunk(x_prompt, mod_p, zero_states, p)
    y_s, st_s = _trunk(x_sample, mod_s,
                       [state_conv_a, state_conv_qkv, state_delta, state_conv_lru, state_lru], p)
    return (y_p, y_s, *st_p, *st_s)
```

```python
import functools

import jax
import jax.numpy as jnp
from jax import lax
from jax.experimental import pallas as pl
from jax.experimental.pallas import tpu as pltpu

F32 = jnp.float32
BF16 = jnp.bfloat16
EPS = 1e-6

V7X_VMEM_LIMIT_BYTES = 56 * 1024 * 1024
LANES = 128
TOKEN_TILE = 512
MOE_ROWS = 256
N_EXPERTS = 8
GDN_CHUNK = 64
LRU_C = 8.0
NEG_BIG = -1e30


def _cparams(sem):
    return pltpu.CompilerParams(dimension_semantics=sem, vmem_limit_bytes=V7X_VMEM_LIMIT_BYTES)


def _sigmoid(x):
    return jax.nn.sigmoid(x)


def _silu(x):
    return x * jax.nn.sigmoid(x)


def _softplus(x):
    return jnp.maximum(x, 0.0) + jnp.log1p(jnp.exp(-jnp.abs(x)))


def _dot(a, b):
    return jnp.dot(a.astype(BF16), b.astype(BF16), preferred_element_type=F32)


def _dot_nt(a, b):
    return lax.dot_general(a.astype(BF16), b.astype(BF16), (((1,), (1,)), ((), ())),
                           preferred_element_type=F32)


def _split2(x):
    hi = x.astype(BF16)
    lo = (x - hi.astype(F32)).astype(BF16)
    return hi, lo


def _split3(x):
    hi = x.astype(BF16)
    r = x - hi.astype(F32)
    mid = r.astype(BF16)
    lo = (r - mid.astype(F32)).astype(BF16)
    return hi, mid, lo


def _dot_hp(a, b):
    ah, al = _split2(a)
    bh, bl = _split2(b)
    d = functools.partial(jnp.dot, preferred_element_type=F32)
    return d(ah, bh) + (d(ah, bl) + d(al, bh))


def _dot_exact_lhs(a_bf16, b):
    d = functools.partial(jnp.dot, preferred_element_type=F32)
    h, m, l = _split3(b)
    return d(a_bf16, h) + (d(a_bf16, m) + d(a_bf16, l))


def _tok_tiles(B, T, target=TOKEN_TILE):
    if T >= target:
        assert T % target == 0
        return 1, target
    bb = max(1, min(B, target // T))
    assert B % bb == 0
    return bb, T


def _ada_kernel(c_ref, w_ref, b_ref, o_ref):
    c = c_ref[...]
    o_ref[...] = _dot(_silu(c), w_ref[...]) + b_ref[...]


def _ada(c_all, w_ada, b_ada):
    L, D, D6 = w_ada.shape
    Bc = c_all.shape[0]
    tn = 1536
    return pl.pallas_call(
        _ada_kernel, name="ada", grid=(L, D6 // tn),
        in_specs=[pl.BlockSpec((Bc, D), lambda l, j: (0, 0)),
                  pl.BlockSpec((None, D, tn), lambda l, j: (l, 0, j)),
                  pl.BlockSpec((None, 1, tn), lambda l, j: (l, 0, j))],
        out_specs=pl.BlockSpec((None, Bc, tn), lambda l, j: (l, 0, j)),
        out_shape=jax.ShapeDtypeStruct((L, Bc, D6), F32),
        compiler_params=_cparams(("parallel", "parallel")),
    )(c_all, w_ada, b_ada.reshape(L, 1, D6))


def _mod_spec(j, bb, D):
    return pl.BlockSpec((None, bb, 1, D), lambda b, t, j=j: (j, b, 0, 0))


_IN_AG = (0, 512)
_IN_QKV = (512, 2048)
_IN_Z = (2048, 2560)
_IN_LRU = (2560, 3072)
_IN_BA = (3072, 3200)
_IN_COLS = 3200


def _inproj_kernel(x_ref, sc_ref, sh_ref, g_ref, w_ref, u_ref, qkv_ref, z_ref, lru_ref, ba_ref):
    bb, tt, D = x_ref.shape
    x = x_ref[...]
    h = x * lax.rsqrt(jnp.mean(x * x, axis=-1, keepdims=True) + EPS) * g_ref[...]
    h = h * (1.0 + sc_ref[...]) + sh_ref[...]
    hb = h.reshape(bb * tt, D).astype(BF16)

    def seg(lo_hi):
        return jnp.dot(hb, w_ref[:, lo_hi[0]:lo_hi[1]], preferred_element_type=F32)

    ag = seg(_IN_AG)
    u_ref[...] = (ag[:, :256] * _sigmoid(ag[:, 256:])).reshape(bb, tt, 256)
    qkv_ref[...] = seg(_IN_QKV).reshape(bb, tt, 1536)
    z_ref[...] = seg(_IN_Z).reshape(bb, tt, 512)
    lru_ref[...] = seg(_IN_LRU).reshape(bb, tt, 512)
    ba_ref[...] = seg(_IN_BA).reshape(bb, tt, LANES)


def _inproj(x, mod, gain, w):
    B, T, D = x.shape
    bb, tt = _tok_tiles(B, T)
    widths = (256, 1536, 512, 512, LANES)
    return pl.pallas_call(
        _inproj_kernel, name="inproj", grid=(B // bb, T // tt),
        in_specs=[pl.BlockSpec((bb, tt, D), lambda b, t: (b, t, 0)),
                  _mod_spec(1, bb, D), _mod_spec(0, bb, D),
                  pl.BlockSpec((1, D), lambda b, t: (0, 0)),
                  pl.BlockSpec((D, _IN_COLS), lambda b, t: (0, 0))],
        out_specs=[pl.BlockSpec((bb, tt, n), lambda b, t: (b, t, 0)) for n in widths],
        out_shape=[jax.ShapeDtypeStruct((B, T, n), F32) for n in widths],
        compiler_params=_cparams(("parallel", "parallel")),
    )(x, mod, mod, gain, w)


def _mixa_kernel(u_ref, buf_ref, w_ref, cb_ref, lg_ref, lb_ref, y_ref, ns_ref, xp_ref):
    bb, tc, ch = u_ref.shape
    width = w_ref.shape[0]
    t = pl.program_id(1)

    @pl.when(t == 0)
    def _():
        xp_ref[:, 2:32, :] = buf_ref[...]

    @pl.when(t > 0)
    def _():
        xp_ref[:, 0:32, :] = xp_ref[:, tc:tc + 32, :]

    xp_ref[:, 32:32 + tc, :] = u_ref[...]
    ns_ref[...] = xp_ref[:, tc + 2:tc + 32, :]

    rs = min(tc, 64)
    for r0 in range(0, tc, rs):
        acc = jnp.zeros((bb, rs, ch), F32) + cb_ref[...]
        for j in range(width):
            acc = acc + w_ref[j:j + 1, :] * xp_ref[:, r0 + 2 + j:r0 + 2 + j + rs, :]
        mu = jnp.mean(acc, axis=-1, keepdims=True)
        d = acc - mu
        var = jnp.mean(d * d, axis=-1, keepdims=True)
        yn = d * lax.rsqrt(var + EPS) * lg_ref[...] + lb_ref[...]
        y_ref[:, r0:r0 + rs, :] = _silu(yn)


def _mixa(u, buf, w, cb, lg, lb):
    B, T, ch = u.shape
    if T >= 256:
        bb, tc = 1, 256
    else:
        bb, tc = min(B, 8), T
    assert T % tc == 0 and B % bb == 0
    nb = buf.shape[1]
    vec = pl.BlockSpec((1, ch), lambda b, t: (0, 0))
    return pl.pallas_call(
        _mixa_kernel, name="mixa", grid=(B // bb, T // tc),
        in_specs=[pl.BlockSpec((bb, tc, ch), lambda b, t: (b, t, 0)),
                  pl.BlockSpec((bb, nb, ch), lambda b, t: (b, 0, 0)),
                  pl.BlockSpec(w.shape, lambda b, t: (0, 0)), vec, vec, vec],
        out_specs=[pl.BlockSpec((bb, tc, ch), lambda b, t: (b, t, 0)),
                   pl.BlockSpec((bb, nb, ch), lambda b, t: (b, 0, 0))],
        out_shape=[jax.ShapeDtypeStruct((B, T, ch), F32), jax.ShapeDtypeStruct((B, nb, ch), F32)],
        scratch_shapes=[pltpu.VMEM((bb, 32 + tc, ch), F32)],
        compiler_params=_cparams(("parallel", "arbitrary")),
    )(u, buf, w, cb, lg, lb)


def _gdn_kernel(qkv_ref, z_ref, ba_ref, sq_ref, sd_ref, cw_ref, alog_ref, dtb_ref, gn_ref,
                y_ref, nq_ref, nd_ref, xp_ref, qn_ref, bg_ref, s_ref, *, C):
    bb, tt, width = qkv_ref.shape
    nh = sd_ref.shape[1]
    dk = sd_ref.shape[2]
    ntap = cw_ref.shape[0]
    t = pl.program_id(1)

    @pl.when(t == 0)
    def _():
        xp_ref[:, 5:8, :] = sq_ref[...]
        s_ref[...] = sd_ref[...]

    @pl.when(t > 0)
    def _():
        xp_ref[:, 5:8, :] = xp_ref[:, tt + 5:tt + 8, :]

    xp_ref[:, 8:8 + tt, :] = qkv_ref[...]
    nq_ref[...] = xp_ref[:, tt + 5:tt + 8, :]

    rs = min(tt, 128)
    for sl in range(width // dk):
        lo = sl * dk
        for r0 in range(0, tt, rs):
            acc = jnp.zeros((bb, rs, dk), F32)
            for j in range(ntap):
                acc = acc + cw_ref[j:j + 1, lo:lo + dk] * xp_ref[:, r0 + 5 + j:r0 + 5 + j + rs, lo:lo + dk]
            s = _silu(acc)
            if sl < 2 * nh:
                s = s * lax.rsqrt(jnp.sum(s * s, axis=-1, keepdims=True) + EPS)
            if sl < nh:
                s = s * (dk ** -0.5)
            qn_ref[:, r0:r0 + rs, lo:lo + dk] = s

    ba = ba_ref[...]
    lane3 = lax.broadcasted_iota(jnp.int32, ba.shape, 2)
    gdec = -jnp.exp(alog_ref[...]) * _softplus(ba + dtb_ref[...])
    bg_ref[...] = jnp.where(lane3 < nh, _sigmoid(ba), jnp.where(lane3 < 2 * nh, gdec, 0.0))

    ii = lax.broadcasted_iota(jnp.int32, (C, C), 0)
    jj = lax.broadcasted_iota(jnp.int32, (C, C), 1)
    tri_incl = jnp.where(ii >= jj, 1.0, 0.0).astype(BF16)
    lane_c = lax.broadcasted_iota(jnp.int32, (C, LANES), 1)
    nsteps = C.bit_length() - 1
    assert (1 << nsteps) == C
    ident = jnp.where(ii == jj, 1.0, 0.0)
    pair_mask = []
    for s in range(nsteps):
        same_big = jnp.right_shift(ii, s + 1) == jnp.right_shift(jj, s + 1)
        diff_small = jnp.right_shift(ii, s) != jnp.right_shift(jj, s)
        pair_mask.append((same_big, diff_small))

    def pair_part(s, m):
        return jnp.where(pair_mask[s][0], jnp.where(pair_mask[s][1], m, 0.0), 0.0)

    nc = tt // C
    probs = [(b, c, h) for b in range(bb) for c in range(nc) for h in range(nh)]
    bgc, gcs, gct = {}, {}, {}
    for b in range(bb):
        for c in range(nc):
            bgc[b, c] = bg_ref[b, c * C:(c + 1) * C, :]
            gcs[b, c] = _dot_exact_lhs(tri_incl, jnp.where(lane_c < nh, 0.0, bgc[b, c]))
    for key in gcs:
        gct[key] = gcs[key].T

    def rows(b, c, slab):
        return qn_ref[b, c * C:(c + 1) * C, slab * dk:(slab + 1) * dk]

    q = {p: rows(p[0], p[1], p[2]) for p in probs}
    k = {p: rows(p[0], p[1], nh + p[2]) for p in probs}
    gcol = {(b, c, h): gcs[b, c][:, nh + h:nh + h + 1] for b, c, h in probs}
    kb = {(b, c, h): k[b, c, h] * bgc[b, c][:, h:h + 1] for b, c, h in probs}
    a2 = {p: _dot_nt(jnp.concatenate([kb[p], q[p]], axis=0), k[p]) for p in probs}
    decay = {(b, c, h): jnp.exp(jnp.where(ii >= jj, gcol[b, c, h] - gct[b, c][nh + h:nh + h + 1, :], -jnp.inf))
             for b, c, h in probs}
    low = {p: jnp.where(ii > jj, a2[p][:C] * decay[p], 0.0) for p in probs}
    tinv = {p: ident - pair_part(0, low[p]) for p in probs}
    for lvl in range(1, nsteps):
        half = {p: _dot(tinv[p], pair_part(lvl, low[p])) for p in probs}
        tinv = {p: tinv[p] - _dot(half[p], tinv[p]) for p in probs}
    rhs = {(b, c, h): jnp.concatenate([rows(b, c, 2 * nh + h) * bgc[b, c][:, h:h + 1],
                                        kb[b, c, h] * jnp.exp(gcol[b, c, h])], axis=1) for b, c, h in probs}
    uw = {p: _dot(tinv[p], rhs[p]) for p in probs}

    state = {(b, h): s_ref[b, h] for b in range(bb) for h in range(nh)}
    for c in range(nc):
        cur = [(b, c, h) for b in range(bb) for h in range(nh)]
        wq = {p: _dot(jnp.concatenate([uw[p][:, dk:], q[p] * jnp.exp(gcol[p])], axis=0), state[p[0], p[2]])
              for p in cur}
        v_new = {p: uw[p][:, :dk] - wq[p][:C] for p in cur}
        o = {p: wq[p][C:] + _dot(a2[p][C:] * decay[p], v_new[p]) for p in cur}
        for b, _, h in cur:
            p = (b, c, h)
            gl = gcs[b, c][C - 1:C, nh + h:nh + h + 1]
            kdec = k[p] * jnp.exp(gl - gcol[p])
            upd = lax.dot_general(kdec.astype(BF16), v_new[p].astype(BF16), (((0,), (0,)), ((), ())),
                                  preferred_element_type=F32)
            state[b, h] = state[b, h] * jnp.exp(gl) + upd
        for b, _, h in cur:
            p = (b, c, h)
            on = o[p] * lax.rsqrt(jnp.mean(o[p] * o[p], axis=-1, keepdims=True) + EPS) * gn_ref[...]
            zz = z_ref[b, c * C:(c + 1) * C, h * dk:(h + 1) * dk]
            y_ref[b, c * C:(c + 1) * C, h * dk:(h + 1) * dk] = on * _silu(zz)
    for (b, h), val in state.items():
        s_ref[b, h] = val
    nd_ref[...] = s_ref[...]


def _gdn(qkv, z, ba, s_qkv, s_delta, cw, alog_vec, dtb_vec, gn):
    B, T, width = qkv.shape
    nh, dk = s_delta.shape[1], s_delta.shape[2]
    C = min(GDN_CHUNK, T)
    if T > C:
        bb = min(B, 2)
        tt = min(T, 8 * C // bb)
    else:
        bb, tt = min(B, 8), T
    assert T % tt == 0 and tt % C == 0 and B % bb == 0
    vec = pl.BlockSpec((1, LANES), lambda b, t: (0, 0))
    kern = functools.partial(_gdn_kernel, C=C)
    return pl.pallas_call(
        kern, name="gdn", grid=(B // bb, T // tt),
        in_specs=[pl.BlockSpec((bb, tt, width), lambda b, t: (b, t, 0)),
                  pl.BlockSpec((bb, tt, nh * dk), lambda b, t: (b, t, 0)),
                  pl.BlockSpec((bb, tt, LANES), lambda b, t: (b, t, 0)),
                  pl.BlockSpec((bb, 3, width), lambda b, t: (b, 0, 0)),
                  pl.BlockSpec((bb, nh, dk, dk), lambda b, t: (b, 0, 0, 0)),
                  pl.BlockSpec(cw.shape, lambda b, t: (0, 0)), vec, vec, vec],
        out_specs=[pl.BlockSpec((bb, tt, nh * dk), lambda b, t: (b, t, 0)),
                   pl.BlockSpec((bb, 3, width), lambda b, t: (b, 0, 0)),
                   pl.BlockSpec((bb, nh, dk, dk), lambda b, t: (b, 0, 0, 0))],
        out_shape=[jax.ShapeDtypeStruct((B, T, nh * dk), F32),
                   jax.ShapeDtypeStruct((B, 3, width), F32),
                   jax.ShapeDtypeStruct((B, nh, dk, dk), F32)],
        scratch_shapes=[pltpu.VMEM((bb, 8 + tt, width), F32), pltpu.VMEM((bb, tt, width), F32),
                        pltpu.VMEM((bb, tt, LANES), F32), pltpu.VMEM((bb, nh, dk, dk), F32)],
        compiler_params=_cparams(("parallel", "arbitrary")),
    )(qkv, z, ba, s_qkv, s_delta, cw, alog_vec, dtb_vec, gn)


def _lru_kernel(lru_ref, cs_ref, h0_ref, cw_ref, cb_ref, wa_ref, ba_ref, wi_ref, bi_ref, lam_ref, ng_ref,
                y_ref, ncs_ref, nh_ref, xp_ref, sa_ref, sb_ref, h_ref):
    bb, tc, two_ch = lru_ref.shape
    ch = two_ch // 2
    ntap = cw_ref.shape[0]
    t = pl.program_id(1)

    @pl.when(t == 0)
    def _():
        xp_ref[:, 5:8, :] = cs_ref[...]
        h_ref[...] = h0_ref[...]

    @pl.when(t > 0)
    def _():
        xp_ref[:, 5:8, :] = xp_ref[:, tc + 5:tc + 8, :]

    xp_ref[:, 8:8 + tc, :] = lru_ref[:, :, 0:ch]
    ncs_ref[...] = xp_ref[:, tc + 5:tc + 8, :]

    xc = jnp.zeros((bb, tc, ch), F32) + cb_ref[...]
    for j in range(ntap):
        xc = xc + cw_ref[j:j + 1, :] * xp_ref[:, 5 + j:5 + j + tc, :]
    x2 = xc.reshape(bb * tc, ch)
    r = _sigmoid(_dot_hp(x2, wa_ref[...]) + ba_ref[...])
    i = _sigmoid(_dot_hp(x2, wi_ref[...]) + bi_ref[...])
    log_a = (-LRU_C * r) * _softplus(-lam_ref[...])
    a = jnp.exp(log_a)
    bv = jnp.sqrt(1.0 - jnp.exp(2.0 * log_a)) * (i * x2)
    a3 = a.reshape(bb, tc, ch)
    b3 = bv.reshape(bb, tc, ch)

    sa_ref[:, 0:tc, :] = jnp.ones((bb, tc, ch), F32)
    sb_ref[:, 0:tc, :] = jnp.zeros((bb, tc, ch), F32)
    d = 1
    while d < tc:
        sa_ref[:, tc:2 * tc, :] = a3
        sb_ref[:, tc:2 * tc, :] = b3
        ap = sa_ref[:, tc - d:2 * tc - d, :]
        bp = sb_ref[:, tc - d:2 * tc - d, :]
        b3 = a3 * bp + b3
        a3 = a3 * ap
        d *= 2
    h = a3 * h_ref[...] + b3
    h_last = h[:, tc - 1:tc, :]
    h_ref[...] = h_last
    nh_ref[...] = h_last

    y = h * jax.nn.gelu(lru_ref[:, :, ch:two_ch])
    y_ref[...] = y * lax.rsqrt(jnp.mean(y * y, axis=-1, keepdims=True) + EPS) * ng_ref[...]


def _lru(lru, cs, h0, cw, cb, wa, ba, wi, bi, lam, ng):
    B, T, two_ch = lru.shape
    ch = two_ch // 2
    if T >= 256:
        bb, tc = 1, 256
    else:
        bb, tc = min(B, 16), T
    assert T % tc == 0 and B % bb == 0
    vec = pl.BlockSpec((1, ch), lambda b, t: (0, 0))
    mat = pl.BlockSpec((ch, ch), lambda b, t: (0, 0))
    return pl.pallas_call(
        _lru_kernel, name="lru", grid=(B // bb, T // tc),
        in_specs=[pl.BlockSpec((bb, tc, two_ch), lambda b, t: (b, t, 0)),
                  pl.BlockSpec((bb, 3, ch), lambda b, t: (b, 0, 0)),
                  pl.BlockSpec((bb, 1, ch), lambda b, t: (b, 0, 0)),
                  pl.BlockSpec(cw.shape, lambda b, t: (0, 0)), vec, mat, vec, mat, vec, vec, vec],
        out_specs=[pl.BlockSpec((bb, tc, ch), lambda b, t: (b, t, 0)),
                   pl.BlockSpec((bb, 3, ch), lambda b, t: (b, 0, 0)),
                   pl.BlockSpec((bb, 1, ch), lambda b, t: (b, 0, 0))],
        out_shape=[jax.ShapeDtypeStruct((B, T, ch), F32), jax.ShapeDtypeStruct((B, 3, ch), F32),
                   jax.ShapeDtypeStruct((B, 1, ch), F32)],
        scratch_shapes=[pltpu.VMEM((bb, 8 + tc, ch), F32), pltpu.VMEM((bb, 2 * tc, ch), F32),
                        pltpu.VMEM((bb, 2 * tc, ch), F32), pltpu.VMEM((bb, 1, ch), F32)],
        compiler_params=_cparams(("parallel", "arbitrary")),
    )(lru, cs, h0, cw, cb, wa, ba, wi, bi, lam, ng)


def _outproj_kernel(ya_ref, yb_ref, yc_ref, x_ref, g1_ref, sc_ref, sh_ref, n2_ref, w_ref, x1_ref, h2_ref):
    bb, tt, D = x_ref.shape
    m = bb * tt
    na, nb = ya_ref.shape[2], yb_ref.shape[2]
    mix = (_dot(ya_ref[...].reshape(m, na), w_ref[0:na, :])
           + _dot(yb_ref[...].reshape(m, nb), w_ref[na:na + nb, :])
           + _dot(yc_ref[...].reshape(m, D - na - nb), w_ref[na + nb:D, :]))
    x1 = x_ref[...] + g1_ref[...] * mix.reshape(bb, tt, D)
    x1_ref[...] = x1
    h = x1 * lax.rsqrt(jnp.mean(x1 * x1, axis=-1, keepdims=True) + EPS) * n2_ref[...]
    h = h * (1.0 + sc_ref[...]) + sh_ref[...]
    h2_ref[...] = h.reshape(m, D)


def _outproj(ya, yb, yc, x, mod, gain2, w):
    B, T, D = x.shape
    bb, tt = _tok_tiles(B, T)
    nt = T // tt

    def tok(n):
        return pl.BlockSpec((bb, tt, n), lambda b, t: (b, t, 0))

    return pl.pallas_call(
        _outproj_kernel, name="outproj", grid=(B // bb, nt),
        in_specs=[tok(ya.shape[2]), tok(yb.shape[2]), tok(yc.shape[2]), tok(D),
                  _mod_spec(2, bb, D), _mod_spec(4, bb, D), _mod_spec(3, bb, D),
                  pl.BlockSpec((1, D), lambda b, t: (0, 0)),
                  pl.BlockSpec((D, D), lambda b, t: (0, 0))],
        out_specs=[tok(D), pl.BlockSpec((bb * tt, D), lambda b, t: (b * nt + t, 0))],
        out_shape=[jax.ShapeDtypeStruct((B, T, D), F32), jax.ShapeDtypeStruct((B * T, D), F32)],
        compiler_params=_cparams(("parallel", "parallel")),
    )(ya, yb, yc, x, mod, mod, mod, gain2, w)


def _ffn_kernel(h_ref, x1_ref, g2_ref, wg_ref, wu_ref, wd_ref, o_ref, acc_ref):
    bb, tt, D = x1_ref.shape
    j = pl.program_id(2)

    @pl.when(j == 0)
    def _():
        acc_ref[...] = jnp.zeros_like(acc_ref)

    hb = h_ref[...].astype(BF16)
    g = jnp.dot(hb, wg_ref[...], preferred_element_type=F32)
    u = jnp.dot(hb, wu_ref[...], preferred_element_type=F32)
    acc_ref[...] += _dot(_silu(g) * u, wd_ref[...])

    @pl.when(j == pl.num_programs(2) - 1)
    def _():
        o_ref[...] = x1_ref[...] + g2_ref[...] * acc_ref[...].reshape(bb, tt, D)


def _ffn(h2, x1, mod, wg, wu, wd):
    B, T, D = x1.shape
    bb, tt = _tok_tiles(B, T)
    nt = T // tt
    F = wg.shape[1]
    tf = F // 2
    assert tf % LANES == 0
    return pl.pallas_call(
        _ffn_kernel, name="ffn", grid=(B // bb, nt, F // tf),
        in_specs=[pl.BlockSpec((bb * tt, D), lambda b, t, j: (b * nt + t, 0)),
                  pl.BlockSpec((bb, tt, D), lambda b, t, j: (b, t, 0)),
                  pl.BlockSpec((None, bb, 1, D), lambda b, t, j: (5, b, 0, 0)),
                  pl.BlockSpec((D, tf), lambda b, t, j: (0, j)),
                  pl.BlockSpec((D, tf), lambda b, t, j: (0, j)),
                  pl.BlockSpec((tf, D), lambda b, t, j: (j, 0))],
        out_specs=pl.BlockSpec((bb, tt, D), lambda b, t, j: (b, t, 0)),
        out_shape=jax.ShapeDtypeStruct((B, T, D), F32),
        scratch_shapes=[pltpu.VMEM((bb * tt, D), F32)],
        compiler_params=_cparams(("parallel", "parallel", "arbitrary")),
    )(h2, x1, mod, wg, wu, wd)


def _router_kernel(h_ref, wr_ref, br_ref, idx_ref, gate_ref, cnt_ref, cnt_acc):
    i = pl.program_id(0)

    @pl.when(i == 0)
    def _():
        cnt_acc[...] = jnp.zeros_like(cnt_acc)

    logits = _dot_hp(h_ref[...], wr_ref[...]) + br_ref[...]
    lane = lax.broadcasted_iota(jnp.int32, logits.shape, 1).astype(F32)
    m1 = jnp.max(logits, axis=-1, keepdims=True)
    i1 = jnp.min(jnp.where(logits == m1, lane, float(LANES)), axis=-1, keepdims=True)
    rest = jnp.where(lane == i1, NEG_BIG, logits)
    m2 = jnp.max(rest, axis=-1, keepdims=True)
    i2 = jnp.min(jnp.where(rest == m2, lane, float(LANES)), axis=-1, keepdims=True)
    e = jnp.exp(m2 - m1)
    g1 = 1.0 / (1.0 + e)
    g2 = e / (1.0 + e)
    idx_ref[...] = jnp.where(lane == 0.0, i1, jnp.where(lane == 1.0, i2, 0.0)).astype(jnp.int32)
    gate_ref[...] = jnp.where(lane == 0.0, g1, jnp.where(lane == 1.0, g2, 0.0))
    oh = jnp.where(lane == i1, 1.0, jnp.where(lane == i2, 1.0, 0.0))
    cnt_acc[...] += jnp.sum(oh, axis=0, keepdims=True)
    cnt_ref[...] = jnp.broadcast_to(cnt_acc[...], cnt_ref.shape)


def _router(h2, wr, br, tm):
    N, D = h2.shape
    return pl.pallas_call(
        _router_kernel, name="router", grid=(N // tm,),
        in_specs=[pl.BlockSpec((tm, D), lambda i: (i, 0)),
                  pl.BlockSpec((D, LANES), lambda i: (0, 0)),
                  pl.BlockSpec((1, LANES), lambda i: (0, 0))],
        out_specs=[pl.BlockSpec((tm, LANES), lambda i: (i, 0)),
                   pl.BlockSpec((tm, LANES), lambda i: (i, 0)),
                   pl.BlockSpec((8, LANES), lambda i: (0, 0))],
        out_shape=[jax.ShapeDtypeStruct((N, LANES), jnp.int32), jax.ShapeDtypeStruct((N, LANES), F32),
                   jax.ShapeDtypeStruct((8, LANES), F32)],
        scratch_shapes=[pltpu.VMEM((1, LANES), F32)],
        compiler_params=_cparams(("arbitrary",)),
    )(h2, wr, br)


def _slot_kernel(idx_ref, pstart_ref, dest_ref, carry):
    i = pl.program_id(0)

    @pl.when(i == 0)
    def _():
        carry[...] = jnp.zeros_like(carry)

    tm = idx_ref.shape[0]
    idx = idx_ref[...]
    lane = lax.broadcasted_iota(jnp.int32, idx.shape, 1)
    oh1 = lane == idx[:, 0:1]
    oh2 = lane == idx[:, 1:2]
    oh = jnp.where(oh1, 1.0, jnp.where(oh2, 1.0, 0.0))
    ii = lax.broadcasted_iota(jnp.int32, (tm, tm), 0)
    jj = lax.broadcasted_iota(jnp.int32, (tm, tm), 1)
    earlier = jnp.dot(jnp.where(ii > jj, 1.0, 0.0).astype(BF16), oh.astype(BF16), preferred_element_type=F32)
    base = earlier + (carry[...] + pstart_ref[...])
    d1 = jnp.sum(jnp.where(oh1, base, 0.0), axis=-1, keepdims=True)
    d2 = jnp.sum(jnp.where(oh2, base, 0.0), axis=-1, keepdims=True)
    dest_ref[...] = jnp.where(lane == 0, d1, jnp.where(lane == 1, d2, 0.0)).astype(jnp.int32)
    carry[...] += jnp.sum(oh, axis=0, keepdims=True)


def _slots(idx, pstart, tm):
    N = idx.shape[0]
    return pl.pallas_call(
        _slot_kernel, name="slots", grid=(N // tm,),
        in_specs=[pl.BlockSpec((tm, LANES), lambda i: (i, 0)), pl.BlockSpec((1, LANES), lambda i: (0, 0))],
        out_specs=pl.BlockSpec((tm, LANES), lambda i: (i, 0)),
        out_shape=jax.ShapeDtypeStruct((N, LANES), jnp.int32),
        scratch_shapes=[pltpu.VMEM((1, LANES), F32)],
        compiler_params=_cparams(("arbitrary",)),
    )(idx, pstart)


def _dispatch_kernel(dest_hbm, h_ref, xs_in, xs_ref, dsm, isem, sem):
    del xs_in
    tm = h_ref.shape[0]
    i = pl.program_id(0)
    cp = pltpu.make_async_copy(dest_hbm.at[i], dsm, isem)
    cp.start()
    cp.wait()

    def row_copy(r, d):
        return pltpu.make_async_copy(h_ref.at[pl.ds(r, 1)], xs_ref.at[pl.ds(d, 1)], sem)

    def issue(r, c):
        row_copy(r, dsm[r]).start()
        row_copy(r, dsm[tm + r]).start()
        return c

    lax.fori_loop(0, tm, issue, 0)

    def drain(r, c):
        row_copy(0, 0).wait()
        row_copy(0, 0).wait()
        return c

    lax.fori_loop(0, tm, drain, 0)


def _dispatch(dest_tiles, h2, rows, tm):
    N, D = h2.shape
    return pl.pallas_call(
        _dispatch_kernel, name="dispatch", grid=(N // tm,),
        in_specs=[pl.BlockSpec(memory_space=pl.ANY),
                  pl.BlockSpec((tm, D), lambda i: (i, 0)),
                  pl.BlockSpec(memory_space=pl.ANY)],
        out_specs=pl.BlockSpec(memory_space=pl.ANY),
        out_shape=jax.ShapeDtypeStruct((rows, D), F32),
        scratch_shapes=[pltpu.SMEM((2 * tm,), jnp.int32), pltpu.SemaphoreType.DMA, pltpu.SemaphoreType.DMA],
        input_output_aliases={2: 0},
        compiler_params=_cparams(("arbitrary",)),
    )(dest_tiles, h2, jnp.zeros((rows, D), F32))


def _expert_kernel(be_ref, nu_ref, xs_ref, wg_ref, wu_ref, wd_ref, ys_ref):
    del be_ref
    used = pl.program_id(0) < nu_ref[0]

    @pl.when(used)
    def _():
        xb = xs_ref[...].astype(BF16)
        g = jnp.dot(xb, wg_ref[...], preferred_element_type=F32)
        u = jnp.dot(xb, wu_ref[...], preferred_element_type=F32)
        ys_ref[...] = _dot(_silu(g) * u, wd_ref[...])

    @pl.when(jnp.logical_not(used))
    def _():
        ys_ref[...] = jnp.zeros_like(ys_ref)


def _experts(block_e, n_used, xs, wg, wu, wd):
    rows, D = xs.shape
    F = wg.shape[2]
    nblk = rows // MOE_ROWS

    def row_map(i, be, nu):
        return (jnp.minimum(i, nu[0] - 1), 0)

    def w_map(i, be, nu):
        return (be[i], 0, 0)

    return pl.pallas_call(
        _expert_kernel, name="experts",
        grid_spec=pltpu.PrefetchScalarGridSpec(
            num_scalar_prefetch=2, grid=(nblk,),
            in_specs=[pl.BlockSpec((MOE_ROWS, D), row_map),
                      pl.BlockSpec((None, D, F), w_map), pl.BlockSpec((None, D, F), w_map),
                      pl.BlockSpec((None, F, D), w_map)],
            out_specs=pl.BlockSpec((MOE_ROWS, D), lambda i, be, nu: (i, 0))),
        out_shape=jax.ShapeDtypeStruct((rows, D), F32),
        compiler_params=_cparams(("arbitrary",)),
    )(block_e, n_used, xs, wg, wu, wd)


def _combine_kernel(dest_hbm, ys_hbm, gate_ref, x1_ref, g2_ref, o_ref, dsm, b0, b1, isem, sem):
    bb, tt, D = x1_ref.shape
    tm = bb * tt
    i = pl.program_id(0) * pl.num_programs(1) + pl.program_id(1)
    cp = pltpu.make_async_copy(dest_hbm.at[i], dsm, isem)
    cp.start()
    cp.wait()

    def row_copy(d, buf, r):
        return pltpu.make_async_copy(ys_hbm.at[pl.ds(d, 1)], buf.at[pl.ds(r, 1)], sem)

    def issue(r, c):
        row_copy(dsm[r], b0, r).start()
        row_copy(dsm[tm + r], b1, r).start()
        return c

    lax.fori_loop(0, tm, issue, 0)

    def drain(r, c):
        row_copy(0, b0, 0).wait()
        row_copy(0, b1, 0).wait()
        return c

    lax.fori_loop(0, tm, drain, 0)
    g = gate_ref[...]
    y = g[:, 0:1] * b0[...] + g[:, 1:2] * b1[...]
    o_ref[...] = x1_ref[...] + g2_ref[...] * y.reshape(bb, tt, D)


def _combine(dest_tiles, ys, gates, x1, mod):
    B, T, D = x1.shape
    bb, tt = _tok_tiles(B, T)
    nt = T // tt
    tm = bb * tt
    return pl.pallas_call(
        _combine_kernel, name="combine", grid=(B // bb, nt),
        in_specs=[pl.BlockSpec(memory_space=pl.ANY), pl.BlockSpec(memory_space=pl.ANY),
                  pl.BlockSpec((tm, LANES), lambda b, t: (b * nt + t, 0)),
                  pl.BlockSpec((bb, tt, D), lambda b, t: (b, t, 0)),
                  _mod_spec(5, bb, D)],
        out_specs=pl.BlockSpec((bb, tt, D), lambda b, t: (b, t, 0)),
        out_shape=jax.ShapeDtypeStruct((B, T, D), F32),
        scratch_shapes=[pltpu.SMEM((2 * tm,), jnp.int32), pltpu.VMEM((tm, D), F32), pltpu.VMEM((tm, D), F32),
                        pltpu.SemaphoreType.DMA, pltpu.SemaphoreType.DMA],
        compiler_params=_cparams(("arbitrary", "arbitrary")),
    )(dest_tiles, ys, gates, x1, mod)


def _moe(h2, x1, mod, wr, br, wg, wu, wd):
    B, T, D = x1.shape
    bb, tt = _tok_tiles(B, T)
    tm = bb * tt
    N = B * T
    idx, gates, cnt = _router(h2, wr, br, tm)
    counts = cnt[0, :N_EXPERTS].astype(jnp.int32)
    padded = (counts + MOE_ROWS - 1) // MOE_ROWS * MOE_ROWS
    pends = jnp.cumsum(padded)
    pstart = pends - padded
    nblk = -(-(2 * N) // MOE_ROWS) + N_EXPERTS
    starts = jnp.arange(nblk, dtype=jnp.int32) * MOE_ROWS
    block_e = jnp.minimum(jnp.sum((pends[None, :] <= starts[:, None]).astype(jnp.int32), axis=1), N_EXPERTS - 1)
    n_used = (pends[-1:] // MOE_ROWS).astype(jnp.int32)
    pstart_vec = jnp.zeros((1, LANES), F32).at[0, :N_EXPERTS].set(pstart.astype(F32))
    dest = _slots(idx, pstart_vec, tm)
    dest_tiles = dest[:, :2].reshape(N // tm, tm, 2).transpose(0, 2, 1).reshape(N // tm, 2 * tm)
    xs = _dispatch(dest_tiles, h2, nblk * MOE_ROWS, tm)
    ys = _experts(block_e, n_used, xs, wg, wu, wd)
    return _combine(dest_tiles, ys, gates, x1, mod)


def _final_kernel(x_ref, g_ref, o_ref):
    x = x_ref[...]
    o_ref[...] = x * lax.rsqrt(jnp.mean(x * x, axis=-1, keepdims=True) + EPS) * g_ref[...]


def _final(x, g):
    B, T, D = x.shape
    bb, tt = _tok_tiles(B, T)
    return pl.pallas_call(
        _final_kernel, name="final", grid=(B // bb, T // tt),
        in_specs=[pl.BlockSpec((bb, tt, D), lambda b, t: (b, t, 0)), pl.BlockSpec((1, D), lambda b, t: (0, 0))],
        out_specs=pl.BlockSpec((bb, tt, D), lambda b, t: (b, t, 0)),
        out_shape=jax.ShapeDtypeStruct((B, T, D), F32),
        compiler_params=_cparams(("parallel", "parallel")),
    )(x, g)


def _trunk(x, mods, st, p):
    L = mods.shape[0]
    new = [[] for _ in range(5)]
    for l in range(L):
        mod = mods[l]
        u, qkv, z, lru, ba = _inproj(x, mod, p['norm1'][l], p['w_in'][l])
        ya, n_a = _mixa(u, st[0][l], p['conv_a_w'][l], p['conv_a_b'][l], p['ln_a_g'][l], p['ln_a_b'][l])
        yb, n_q, n_d = _gdn(qkv, z, ba, st[1][l], st[2][l], p['conv_qkv_w'][l], p['alog'][l], p['dtb'][l],
                            p['gdn_norm'][l])
        yc, n_cl, n_h = _lru(lru, st[3][l], st[4][l][:, None, :], p['conv_lru_w'][l], p['conv_lru_b'][l],
                             p['lru_wa'][l], p['lru_b_a'][l], p['lru_wi'][l], p['lru_b_i'][l],
                             p['lru_lambda'][l], p['lru_norm'][l])
        x1, h2 = _outproj(ya, yb, yc, x, mod, p['norm2'][l], p['w_out'][l])
        j = l // 2
        if l % 2 == 0:
            x = _ffn(h2, x1, mod, p['ffn_wg'][j], p['ffn_wu'][j], p['ffn_wd'][j])
        else:
            x = _moe(h2, x1, mod, p['router_w'][j], p['router_b'][j], p['moe_wg'][j], p['moe_wu'][j],
                     p['moe_wd'][j])
        for lst, val in zip(new, (n_a, n_q, n_d, n_cl, n_h[:, 0, :])):
            lst.append(val)
    y = _final(x, p['final_norm'])
    return y, [jnp.stack(v, axis=0) for v in new]


def kernel(x_prompt, x_sample, c_prompt, c_sample, state_conv_a, state_conv_qkv, state_delta, state_conv_lru, state_lru, w_ada, b_ada, norm1, norm2, w_in, w_out, conv_a_w, conv_a_b, ln_a_g, ln_a_b, conv_qkv_w, gdn_a_log, gdn_dt_bias, gdn_norm, conv_lru_w, conv_lru_b, lru_w_a, lru_b_a, lru_w_i, lru_b_i, lru_lambda, lru_norm, ffn_wg, ffn_wu, ffn_wd, router_w, router_b, moe_wg, moe_wu, moe_wd, final_norm):
    L, D = norm1.shape
    Bp, Bs = x_prompt.shape[0], x_sample.shape[0]
    nh = gdn_a_log.shape[1]

    def row(v):
        return v[:, None, :]

    def lanes(v):
        return jnp.zeros((L, 1, LANES), F32).at[:, 0, nh:2 * nh].set(v)

    def block_diag(w):
        n, k = w.shape[1], w.shape[2]
        eye = jnp.eye(n, dtype=w.dtype)
        return jnp.einsum('lnij,nm->lnimj', w, eye).reshape(L, n * k, n * k)

    split = 2 * 256 + 3 * 512 + 512
    w_in_r = jnp.concatenate([w_in[:, :, :split], w_in[:, :, split + 2 * nh:], w_in[:, :, split:split + 2 * nh],
                              jnp.zeros((L, D, LANES - 2 * nh), w_in.dtype)], axis=-1).astype(BF16)
    n_moe = router_w.shape[0]
    p = dict(
        norm1=row(norm1), norm2=row(norm2), w_in=w_in_r, w_out=w_out.astype(BF16),
        conv_a_w=conv_a_w, conv_a_b=row(conv_a_b), ln_a_g=row(ln_a_g), ln_a_b=row(ln_a_b),
        conv_qkv_w=conv_qkv_w, alog=lanes(gdn_a_log), dtb=lanes(gdn_dt_bias), gdn_norm=row(gdn_norm),
        conv_lru_w=conv_lru_w, conv_lru_b=row(conv_lru_b), lru_wa=block_diag(lru_w_a), lru_b_a=row(lru_b_a),
        lru_wi=block_diag(lru_w_i), lru_b_i=row(lru_b_i), lru_lambda=row(lru_lambda), lru_norm=row(lru_norm),
        ffn_wg=ffn_wg.astype(BF16), ffn_wu=ffn_wu.astype(BF16), ffn_wd=ffn_wd.astype(BF16),
        router_w=jnp.concatenate([router_w, jnp.zeros((n_moe, D, LANES - N_EXPERTS), F32)], axis=-1),
        router_b=jnp.concatenate([router_b, jnp.full((n_moe, LANES - N_EXPERTS), NEG_BIG, F32)],
                                 axis=-1)[:, None, :],
        moe_wg=moe_wg.astype(BF16), moe_wu=moe_wu.astype(BF16), moe_wd=moe_wd.astype(BF16),
        final_norm=final_norm[None, :],
    )

    c_all = jnp.concatenate([c_prompt, c_sample], axis=0)
    mod = _ada(c_all, w_ada, b_ada)
    mod = mod.reshape(L, Bp + Bs, 6, D).transpose(0, 2, 1, 3)[:, :, :, None, :]
    mod_p, mod_s = mod[:, :, :Bp], mod[:, :, Bp:]

    zero_states = [jnp.zeros((L, Bp) + s.shape[2:], F32)
                   for s in (state_conv_a, state_conv_qkv, state_delta, state_conv_lru, state_lru)]
    y_p, st_p = _trunk(x_prompt, mod_p, zero_states, p)
    y_s, st_s = _trunk(x_sample, mod_s,
                       [state_conv_a, state_conv_qkv, state_delta, state_conv_lru, state_lru], p)
    return (y_p, y_s, *st_p, *st_s)
```

```python
import functools

import jax
import jax.numpy as jnp
from jax import lax
from jax.experimental import pallas as pl
from jax.experimental.pallas import tpu as pltpu

F32 = jnp.float32
BF16 = jnp.bfloat16
EPS = 1e-6

V7X_VMEM_LIMIT_BYTES = 56 * 1024 * 1024
LANES = 128
TOKEN_TILE = 512
MOE_ROWS = 256
N_EXPERTS = 8
GDN_CHUNK = 64
LRU_C = 8.0
NEG_BIG = -1e30


def _cparams(sem):
    return pltpu.CompilerParams(dimension_semantics=sem, vmem_limit_bytes=V7X_VMEM_LIMIT_BYTES)


def _sigmoid(x):
    return jax.nn.sigmoid(x)


def _silu(x):
    return x * jax.nn.sigmoid(x)


def _softplus(x):
    return jnp.maximum(x, 0.0) + jnp.log1p(jnp.exp(-jnp.abs(x)))


def _dot(a, b):
    return jnp.dot(a.astype(BF16), b.astype(BF16), preferred_element_type=F32)


def _dot_nt(a, b):
    return lax.dot_general(a.astype(BF16), b.astype(BF16), (((1,), (1,)), ((), ())),
                           preferred_element_type=F32)


def _split2(x):
    hi = x.astype(BF16)
    lo = (x - hi.astype(F32)).astype(BF16)
    return hi, lo


def _split3(x):
    hi = x.astype(BF16)
    r = x - hi.astype(F32)
    mid = r.astype(BF16)
    lo = (r - mid.astype(F32)).astype(BF16)
    return hi, mid, lo


def _dot_hp(a, b):
    ah, al = _split2(a)
    bh, bl = _split2(b)
    d = functools.partial(jnp.dot, preferred_element_type=F32)
    return d(ah, bh) + (d(ah, bl) + d(al, bh))


def _dot_exact_lhs(a_bf16, b):
    d = functools.partial(jnp.dot, preferred_element_type=F32)
    h, m, l = _split3(b)
    return d(a_bf16, h) + (d(a_bf16, m) + d(a_bf16, l))


def _tok_tiles(B, T, target=TOKEN_TILE):
    if T >= target:
        assert T % target == 0
        return 1, target
    bb = max(1, min(B, target // T))
    assert B % bb == 0
    return bb, T


def _ada_kernel(c_ref, w_ref, b_ref, o_ref):
    c = c_ref[...]
    o_ref[...] = _dot(_silu(c), w_ref[...]) + b_ref[...]


def _ada(c_all, w_ada, b_ada):
    L, D, D6 = w_ada.shape
    Bc = c_all.shape[0]
    tn = 1536
    return pl.pallas_call(
        _ada_kernel, name="ada", grid=(L, D6 // tn),
        in_specs=[pl.BlockSpec((Bc, D), lambda l, j: (0, 0)),
                  pl.BlockSpec((None, D, tn), lambda l, j: (l, 0, j)),
                  pl.BlockSpec((None, 1, tn), lambda l, j: (l, 0, j))],
        out_specs=pl.BlockSpec((None, Bc, tn), lambda l, j: (l, 0, j)),
        out_shape=jax.ShapeDtypeStruct((L, Bc, D6), F32),
        compiler_params=_cparams(("parallel", "parallel")),
    )(c_all, w_ada, b_ada.reshape(L, 1, D6))


def _mod_spec(l, j, bb, D):
    return pl.BlockSpec((None, None, bb, 1, D), lambda b, *_: (l, j, b, 0, 0))


def _layer_spec(l, shape):
    zeros = (0,) * len(shape)
    return pl.BlockSpec((None,) + tuple(shape), lambda *_: (l,) + zeros)


def _state_spec(l, bb, shape):
    zeros = (0,) * len(shape)
    return pl.BlockSpec((None, bb) + tuple(shape), lambda b, *_: (l, b) + zeros)


_IN_AG = (0, 512)
_IN_QKV = (512, 2048)
_IN_Z = (2048, 2560)
_IN_LRU = (2560, 3072)
_IN_BA = (3072, 3200)
_IN_COLS = 3200


def _inproj_kernel(x_ref, sc_ref, sh_ref, g_ref, w_ref, u_ref, qkv_ref, z_ref, lru_ref, ba_ref):
    bb, tt, D = x_ref.shape
    x = x_ref[...]
    h = x * lax.rsqrt(jnp.mean(x * x, axis=-1, keepdims=True) + EPS) * g_ref[...]
    h = h * (1.0 + sc_ref[...]) + sh_ref[...]
    hb = h.reshape(bb * tt, D).astype(BF16)

    def seg(lo_hi):
        return jnp.dot(hb, w_ref[:, lo_hi[0]:lo_hi[1]], preferred_element_type=F32)

    ag = seg(_IN_AG)
    u_ref[...] = (ag[:, :256] * _sigmoid(ag[:, 256:])).reshape(bb, tt, 256)
    qkv_ref[...] = seg(_IN_QKV).reshape(bb, tt, 1536)
    z_ref[...] = seg(_IN_Z).reshape(bb, tt, 512)
    lru_ref[...] = seg(_IN_LRU).reshape(bb, tt, 512)
    ba_ref[...] = seg(_IN_BA).reshape(bb, tt, LANES)


def _inproj(x, mod, gain, w, l):
    B, T, D = x.shape
    bb, tt = _tok_tiles(B, T)
    widths = (256, 1536, 512, 512, LANES)
    return pl.pallas_call(
        _inproj_kernel, name="inproj", grid=(B // bb, T // tt),
        in_specs=[pl.BlockSpec((bb, tt, D), lambda b, t: (b, t, 0)),
                  _mod_spec(l, 1, bb, D), _mod_spec(l, 0, bb, D),
                  _layer_spec(l, (1, D)), _layer_spec(l, (D, _IN_COLS))],
        out_specs=[pl.BlockSpec((bb, tt, n), lambda b, t: (b, t, 0)) for n in widths],
        out_shape=[jax.ShapeDtypeStruct((B, T, n), F32) for n in widths],
        compiler_params=_cparams(("parallel", "parallel")),
    )(x, mod, mod, gain, w)


def _mixa_kernel(u_ref, buf_ref, w_ref, cb_ref, lg_ref, lb_ref, y_ref, ns_ref, xp_ref):
    bb, tc, ch = u_ref.shape
    width = w_ref.shape[0]
    t = pl.program_id(1)

    @pl.when(t == 0)
    def _():
        xp_ref[:, 2:32, :] = buf_ref[...]

    @pl.when(t > 0)
    def _():
        xp_ref[:, 0:32, :] = xp_ref[:, tc:tc + 32, :]

    xp_ref[:, 32:32 + tc, :] = u_ref[...]
    ns_ref[...] = xp_ref[:, tc + 2:tc + 32, :]

    rs = min(tc, 64)
    for r0 in range(0, tc, rs):
        acc = jnp.zeros((bb, rs, ch), F32) + cb_ref[...]
        for j in range(width):
            acc = acc + w_ref[j:j + 1, :] * xp_ref[:, r0 + 2 + j:r0 + 2 + j + rs, :]
        mu = jnp.mean(acc, axis=-1, keepdims=True)
        d = acc - mu
        var = jnp.mean(d * d, axis=-1, keepdims=True)
        yn = d * lax.rsqrt(var + EPS) * lg_ref[...] + lb_ref[...]
        y_ref[:, r0:r0 + rs, :] = _silu(yn)


def _mixa(u, buf, w, cb, lg, lb, l):
    B, T, ch = u.shape
    if T >= 256:
        bb, tc = 1, 256
    else:
        bb, tc = min(B, 8), T
    assert T % tc == 0 and B % bb == 0
    nb = buf.shape[2]
    vec = _layer_spec(l, (1, ch))
    return pl.pallas_call(
        _mixa_kernel, name="mixa", grid=(B // bb, T // tc),
        in_specs=[pl.BlockSpec((bb, tc, ch), lambda b, t: (b, t, 0)),
                  _state_spec(l, bb, (nb, ch)),
                  _layer_spec(l, w.shape[1:]), vec, vec, vec],
        out_specs=[pl.BlockSpec((bb, tc, ch), lambda b, t: (b, t, 0)),
                   pl.BlockSpec((bb, nb, ch), lambda b, t: (b, 0, 0))],
        out_shape=[jax.ShapeDtypeStruct((B, T, ch), F32), jax.ShapeDtypeStruct((B, nb, ch), F32)],
        scratch_shapes=[pltpu.VMEM((bb, 32 + tc, ch), F32)],
        compiler_params=_cparams(("parallel", "arbitrary")),
    )(u, buf, w, cb, lg, lb)


def _gdn_kernel(qkv_ref, z_ref, ba_ref, sq_ref, sd_ref, cw_ref, alog_ref, dtb_ref, gn_ref,
                y_ref, nq_ref, nd_ref, xp_ref, qn_ref, bg_ref, s_ref, *, C):
    bb, tt, width = qkv_ref.shape
    nh = sd_ref.shape[1]
    dk = sd_ref.shape[2]
    ntap = cw_ref.shape[0]
    t = pl.program_id(1)

    @pl.when(t == 0)
    def _():
        xp_ref[:, 5:8, :] = sq_ref[...]
        s_ref[...] = sd_ref[...]

    @pl.when(t > 0)
    def _():
        xp_ref[:, 5:8, :] = xp_ref[:, tt + 5:tt + 8, :]

    xp_ref[:, 8:8 + tt, :] = qkv_ref[...]
    nq_ref[...] = xp_ref[:, tt + 5:tt + 8, :]

    rs = min(tt, 128)
    for sl in range(width // dk):
        lo = sl * dk
        for r0 in range(0, tt, rs):
            acc = jnp.zeros((bb, rs, dk), F32)
            for j in range(ntap):
                acc = acc + cw_ref[j:j + 1, lo:lo + dk] * xp_ref[:, r0 + 5 + j:r0 + 5 + j + rs, lo:lo + dk]
            s = _silu(acc)
            if sl < 2 * nh:
                s = s * lax.rsqrt(jnp.sum(s * s, axis=-1, keepdims=True) + EPS)
            if sl < nh:
                s = s * (dk ** -0.5)
            qn_ref[:, r0:r0 + rs, lo:lo + dk] = s

    ba = ba_ref[...]
    lane3 = lax.broadcasted_iota(jnp.int32, ba.shape, 2)
    gdec = -jnp.exp(alog_ref[...]) * _softplus(ba + dtb_ref[...])
    bg_ref[...] = jnp.where(lane3 < nh, _sigmoid(ba), jnp.where(lane3 < 2 * nh, gdec, 0.0))

    ii = lax.broadcasted_iota(jnp.int32, (C, C), 0)
    jj = lax.broadcasted_iota(jnp.int32, (C, C), 1)
    tri_incl = jnp.where(ii >= jj, 1.0, 0.0).astype(BF16)
    lane_c = lax.broadcasted_iota(jnp.int32, (C, LANES), 1)
    nsteps = C.bit_length() - 1
    assert (1 << nsteps) == C
    ident = jnp.where(ii == jj, 1.0, 0.0)
    pair_mask = []
    for s in range(nsteps):
        same_big = jnp.right_shift(ii, s + 1) == jnp.right_shift(jj, s + 1)
        diff_small = jnp.right_shift(ii, s) != jnp.right_shift(jj, s)
        pair_mask.append((same_big, diff_small))

    def pair_part(s, m):
        return jnp.where(pair_mask[s][0], jnp.where(pair_mask[s][1], m, 0.0), 0.0)

    nc = tt // C
    probs = [(b, c, h) for b in range(bb) for c in range(nc) for h in range(nh)]
    bgc, gcs, gct = {}, {}, {}
    for b in range(bb):
        for c in range(nc):
            bgc[b, c] = bg_ref[b, c * C:(c + 1) * C, :]
            gcs[b, c] = _dot_exact_lhs(tri_incl, jnp.where(lane_c < nh, 0.0, bgc[b, c]))
    for key in gcs:
        gct[key] = gcs[key].T

    def rows(b, c, slab):
        return qn_ref[b, c * C:(c + 1) * C, slab * dk:(slab + 1) * dk]

    q = {p: rows(p[0], p[1], p[2]) for p in probs}
    k = {p: rows(p[0], p[1], nh + p[2]) for p in probs}
    gcol = {(b, c, h): gcs[b, c][:, nh + h:nh + h + 1] for b, c, h in probs}
    kb = {(b, c, h): k[b, c, h] * bgc[b, c][:, h:h + 1] for b, c, h in probs}
    a2 = {p: _dot_nt(jnp.concatenate([kb[p], q[p]], axis=0), k[p]) for p in probs}
    decay = {(b, c, h): jnp.exp(jnp.where(ii >= jj, gcol[b, c, h] - gct[b, c][nh + h:nh + h + 1, :], -jnp.inf))
             for b, c, h in probs}
    low = {p: jnp.where(ii > jj, a2[p][:C] * decay[p], 0.0) for p in probs}
    tinv = {p: ident - pair_part(0, low[p]) for p in probs}
    for lvl in range(1, nsteps):
        half = {p: _dot(tinv[p], pair_part(lvl, low[p])) for p in probs}
        tinv = {p: tinv[p] - _dot(half[p], tinv[p]) for p in probs}
    rhs = {(b, c, h): jnp.concatenate([rows(b, c, 2 * nh + h) * bgc[b, c][:, h:h + 1],
                                        kb[b, c, h] * jnp.exp(gcol[b, c, h])], axis=1) for b, c, h in probs}
    uw = {p: _dot(tinv[p], rhs[p]) for p in probs}

    state = {(b, h): s_ref[b, h] for b in range(bb) for h in range(nh)}
    for c in range(nc):
        cur = [(b, c, h) for b in range(bb) for h in range(nh)]
        wq = {p: _dot(jnp.concatenate([uw[p][:, dk:], q[p] * jnp.exp(gcol[p])], axis=0), state[p[0], p[2]])
              for p in cur}
        v_new = {p: uw[p][:, :dk] - wq[p][:C] for p in cur}
        o = {p: wq[p][C:] + _dot(a2[p][C:] * decay[p], v_new[p]) for p in cur}
        for b, _, h in cur:
            p = (b, c, h)
            gl = gcs[b, c][C - 1:C, nh + h:nh + h + 1]
            kdec = k[p] * jnp.exp(gl - gcol[p])
            upd = lax.dot_general(kdec.astype(BF16), v_new[p].astype(BF16), (((0,), (0,)), ((), ())),
                                  preferred_element_type=F32)
            state[b, h] = state[b, h] * jnp.exp(gl) + upd
        for b, _, h in cur:
            p = (b, c, h)
            on = o[p] * lax.rsqrt(jnp.mean(o[p] * o[p], axis=-1, keepdims=True) + EPS) * gn_ref[...]
            zz = z_ref[b, c * C:(c + 1) * C, h * dk:(h + 1) * dk]
            y_ref[b, c * C:(c + 1) * C, h * dk:(h + 1) * dk] = on * _silu(zz)
    for (b, h), val in state.items():
        s_ref[b, h] = val
    nd_ref[...] = s_ref[...]


def _gdn(qkv, z, ba, s_qkv, s_delta, cw, alog_vec, dtb_vec, gn, l):
    B, T, width = qkv.shape
    nh, dk = s_delta.shape[2], s_delta.shape[3]
    C = min(GDN_CHUNK, T)
    if T > C:
        bb = min(B, 2)
        tt = min(T, 8 * C // bb)
    else:
        bb, tt = min(B, 8), T
    assert T % tt == 0 and tt % C == 0 and B % bb == 0
    vec = _layer_spec(l, (1, LANES))
    kern = functools.partial(_gdn_kernel, C=C)
    return pl.pallas_call(
        kern, name="gdn", grid=(B // bb, T // tt),
        in_specs=[pl.BlockSpec((bb, tt, width), lambda b, t: (b, t, 0)),
                  pl.BlockSpec((bb, tt, nh * dk), lambda b, t: (b, t, 0)),
                  pl.BlockSpec((bb, tt, LANES), lambda b, t: (b, t, 0)),
                  _state_spec(l, bb, (3, width)),
                  _state_spec(l, bb, (nh, dk, dk)),
                  _layer_spec(l, cw.shape[1:]), vec, vec, vec],
        out_specs=[pl.BlockSpec((bb, tt, nh * dk), lambda b, t: (b, t, 0)),
                   pl.BlockSpec((bb, 3, width), lambda b, t: (b, 0, 0)),
                   pl.BlockSpec((bb, nh, dk, dk), lambda b, t: (b, 0, 0, 0))],
        out_shape=[jax.ShapeDtypeStruct((B, T, nh * dk), F32),
                   jax.ShapeDtypeStruct((B, 3, width), F32),
                   jax.ShapeDtypeStruct((B, nh, dk, dk), F32)],
        scratch_shapes=[pltpu.VMEM((bb, 8 + tt, width), F32), pltpu.VMEM((bb, tt, width), F32),
                        pltpu.VMEM((bb, tt, LANES), F32), pltpu.VMEM((bb, nh, dk, dk), F32)],
        compiler_params=_cparams(("parallel", "arbitrary")),
    )(qkv, z, ba, s_qkv, s_delta, cw, alog_vec, dtb_vec, gn)


def _lru_kernel(lru_ref, cs_ref, h0_ref, cw_ref, cb_ref, wa_ref, ba_ref, wi_ref, bi_ref, lam_ref, ng_ref,
                y_ref, ncs_ref, nh_ref, xp_ref, sa_ref, sb_ref, h_ref):
    bb, tc, two_ch = lru_ref.shape
    ch = two_ch // 2
    ntap = cw_ref.shape[0]
    t = pl.program_id(1)

    @pl.when(t == 0)
    def _():
        xp_ref[:, 5:8, :] = cs_ref[...]
        h_ref[...] = h0_ref[...]

    @pl.when(t > 0)
    def _():
        xp_ref[:, 5:8, :] = xp_ref[:, tc + 5:tc + 8, :]

    xp_ref[:, 8:8 + tc, :] = lru_ref[:, :, 0:ch]
    ncs_ref[...] = xp_ref[:, tc + 5:tc + 8, :]

    xc = jnp.zeros((bb, tc, ch), F32) + cb_ref[...]
    for j in range(ntap):
        xc = xc + cw_ref[j:j + 1, :] * xp_ref[:, 5 + j:5 + j + tc, :]
    x2 = xc.reshape(bb * tc, ch)
    r = _sigmoid(_dot_hp(x2, wa_ref[...]) + ba_ref[...])
    i = _sigmoid(_dot_hp(x2, wi_ref[...]) + bi_ref[...])
    log_a = (-LRU_C * r) * _softplus(-lam_ref[...])
    a = jnp.exp(log_a)
    bv = jnp.sqrt(1.0 - jnp.exp(2.0 * log_a)) * (i * x2)
    a3 = a.reshape(bb, tc, ch)
    b3 = bv.reshape(bb, tc, ch)

    sa_ref[:, 0:tc, :] = jnp.ones((bb, tc, ch), F32)
    sb_ref[:, 0:tc, :] = jnp.zeros((bb, tc, ch), F32)
    d = 1
    while d < tc:
        sa_ref[:, tc:2 * tc, :] = a3
        sb_ref[:, tc:2 * tc, :] = b3
        ap = sa_ref[:, tc - d:2 * tc - d, :]
        bp = sb_ref[:, tc - d:2 * tc - d, :]
        b3 = a3 * bp + b3
        a3 = a3 * ap
        d *= 2
    h = a3 * h_ref[...] + b3
    h_last = h[:, tc - 1:tc, :]
    h_ref[...] = h_last
    nh_ref[...] = h_last

    y = h * jax.nn.gelu(lru_ref[:, :, ch:two_ch])
    y_ref[...] = y * lax.rsqrt(jnp.mean(y * y, axis=-1, keepdims=True) + EPS) * ng_ref[...]


def _lru(lru, cs, h0, cw, cb, wa, ba, wi, bi, lam, ng, l):
    B, T, two_ch = lru.shape
    ch = two_ch // 2
    if T >= 256:
        bb, tc = 1, 256
    else:
        bb, tc = min(B, 16), T
    assert T % tc == 0 and B % bb == 0
    vec = _layer_spec(l, (1, ch))
    mat = _layer_spec(l, (ch, ch))
    return pl.pallas_call(
        _lru_kernel, name="lru", grid=(B // bb, T // tc),
        in_specs=[pl.BlockSpec((bb, tc, two_ch), lambda b, t: (b, t, 0)),
                  _state_spec(l, bb, (3, ch)),
                  _state_spec(l, bb, (1, ch)),
                  _layer_spec(l, cw.shape[1:]), vec, mat, vec, mat, vec, vec, vec],
        out_specs=[pl.BlockSpec((bb, tc, ch), lambda b, t: (b, t, 0)),
                   pl.BlockSpec((bb, 3, ch), lambda b, t: (b, 0, 0)),
                   pl.BlockSpec((bb, 1, ch), lambda b, t: (b, 0, 0))],
        out_shape=[jax.ShapeDtypeStruct((B, T, ch), F32), jax.ShapeDtypeStruct((B, 3, ch), F32),
                   jax.ShapeDtypeStruct((B, 1, ch), F32)],
        scratch_shapes=[pltpu.VMEM((bb, 8 + tc, ch), F32), pltpu.VMEM((bb, 2 * tc, ch), F32),
                        pltpu.VMEM((bb, 2 * tc, ch), F32), pltpu.VMEM((bb, 1, ch), F32)],
        compiler_params=_cparams(("parallel", "arbitrary")),
    )(lru, cs, h0, cw, cb, wa, ba, wi, bi, lam, ng)


def _outproj_kernel(ya_ref, yb_ref, yc_ref, x_ref, g1_ref, sc_ref, sh_ref, n2_ref, w_ref, x1_ref, h2_ref):
    bb, tt, D = x_ref.shape
    m = bb * tt
    na, nb = ya_ref.shape[2], yb_ref.shape[2]
    mix = (_dot(ya_ref[...].reshape(m, na), w_ref[0:na, :])
           + _dot(yb_ref[...].reshape(m, nb), w_ref[na:na + nb, :])
           + _dot(yc_ref[...].reshape(m, D - na - nb), w_ref[na + nb:D, :]))
    x1 = x_ref[...] + g1_ref[...] * mix.reshape(bb, tt, D)
    x1_ref[...] = x1
    h = x1 * lax.rsqrt(jnp.mean(x1 * x1, axis=-1, keepdims=True) + EPS) * n2_ref[...]
    h = (h * (1.0 + sc_ref[...]) + sh_ref[...]).reshape(m, D)
    if len(h2_ref.shape) == 2:
        h2_ref[...] = h
    else:
        _to_token_tiles(h, h2_ref)


def _to_token_tiles(val, ref):
    for s in range(ref.shape[1]):
        ref[:, s, :] = val[:, s * LANES:(s + 1) * LANES]


def _from_token_tiles(ref):
    return jnp.concatenate([ref[:, s, :] for s in range(ref.shape[1])], axis=1)


def _outproj(ya, yb, yc, x, mod, gain2, w, l, tile_out):
    B, T, D = x.shape
    bb, tt = _tok_tiles(B, T)
    nt = T // tt

    def tok(n):
        return pl.BlockSpec((bb, tt, n), lambda b, t: (b, t, 0))

    if tile_out:
        h2_spec = pl.BlockSpec((bb * tt, D // LANES, LANES), lambda b, t: (b * nt + t, 0, 0))
        h2_shape = jax.ShapeDtypeStruct((B * T, D // LANES, LANES), F32)
    else:
        h2_spec = pl.BlockSpec((bb * tt, D), lambda b, t: (b * nt + t, 0))
        h2_shape = jax.ShapeDtypeStruct((B * T, D), F32)
    return pl.pallas_call(
        _outproj_kernel, name="outproj", grid=(B // bb, nt),
        in_specs=[tok(ya.shape[2]), tok(yb.shape[2]), tok(yc.shape[2]), tok(D),
                  _mod_spec(l, 2, bb, D), _mod_spec(l, 4, bb, D), _mod_spec(l, 3, bb, D),
                  _layer_spec(l, (1, D)), _layer_spec(l, (D, D))],
        out_specs=[tok(D), h2_spec],
        out_shape=[jax.ShapeDtypeStruct((B, T, D), F32), h2_shape],
        compiler_params=_cparams(("parallel", "parallel")),
    )(ya, yb, yc, x, mod, mod, mod, gain2, w)


def _ffn_kernel(h_ref, x1_ref, g2_ref, wg_ref, wu_ref, wd_ref, o_ref, acc_ref):
    bb, tt, D = x1_ref.shape
    j = pl.program_id(2)

    @pl.when(j == 0)
    def _():
        acc_ref[...] = jnp.zeros_like(acc_ref)

    hb = h_ref[...].astype(BF16)
    g = jnp.dot(hb, wg_ref[...], preferred_element_type=F32)
    u = jnp.dot(hb, wu_ref[...], preferred_element_type=F32)
    acc_ref[...] += _dot(_silu(g) * u, wd_ref[...])

    @pl.when(j == pl.num_programs(2) - 1)
    def _():
        o_ref[...] = x1_ref[...] + g2_ref[...] * acc_ref[...].reshape(bb, tt, D)


def _ffn(h2, x1, mod, wg, wu, wd, l, jl):
    B, T, D = x1.shape
    bb, tt = _tok_tiles(B, T)
    nt = T // tt
    F = wg.shape[2]
    tf = F // 2
    assert tf % LANES == 0
    return pl.pallas_call(
        _ffn_kernel, name="ffn", grid=(B // bb, nt, F // tf),
        in_specs=[pl.BlockSpec((bb * tt, D), lambda b, t, j: (b * nt + t, 0)),
                  pl.BlockSpec((bb, tt, D), lambda b, t, j: (b, t, 0)),
                  _mod_spec(l, 5, bb, D),
                  pl.BlockSpec((None, D, tf), lambda b, t, j: (jl, 0, j)),
                  pl.BlockSpec((None, D, tf), lambda b, t, j: (jl, 0, j)),
                  pl.BlockSpec((None, tf, D), lambda b, t, j: (jl, j, 0))],
        out_specs=pl.BlockSpec((bb, tt, D), lambda b, t, j: (b, t, 0)),
        out_shape=jax.ShapeDtypeStruct((B, T, D), F32),
        scratch_shapes=[pltpu.VMEM((bb * tt, D), F32)],
        compiler_params=_cparams(("parallel", "parallel", "arbitrary")),
    )(h2, x1, mod, wg, wu, wd)


def _router_kernel(h_ref, wr_ref, br_ref, idx_ref, gate_ref, cnt_ref, cnt_acc):
    i = pl.program_id(0)

    @pl.when(i == 0)
    def _():
        cnt_acc[...] = jnp.zeros_like(cnt_acc)

    logits = _dot_hp(_from_token_tiles(h_ref), wr_ref[...]) + br_ref[...]
    lane = lax.broadcasted_iota(jnp.int32, logits.shape, 1).astype(F32)
    m1 = jnp.max(logits, axis=-1, keepdims=True)
    i1 = jnp.min(jnp.where(logits == m1, lane, float(LANES)), axis=-1, keepdims=True)
    rest = jnp.where(lane == i1, NEG_BIG, logits)
    m2 = jnp.max(rest, axis=-1, keepdims=True)
    i2 = jnp.min(jnp.where(rest == m2, lane, float(LANES)), axis=-1, keepdims=True)
    e = jnp.exp(m2 - m1)
    g1 = 1.0 / (1.0 + e)
    g2 = e / (1.0 + e)
    idx_ref[...] = jnp.where(lane == 0.0, i1, jnp.where(lane == 1.0, i2, 0.0)).astype(jnp.int32)
    gate_ref[...] = jnp.where(lane == 0.0, g1, jnp.where(lane == 1.0, g2, 0.0))
    oh = jnp.where(lane == i1, 1.0, jnp.where(lane == i2, 1.0, 0.0))
    cnt_acc[...] += jnp.sum(oh, axis=0, keepdims=True)
    cnt_ref[...] = jnp.broadcast_to(cnt_acc[...], cnt_ref.shape)


def _router(h2, wr, br, tm, jl):
    N, ns, _ = h2.shape
    D = ns * LANES
    return pl.pallas_call(
        _router_kernel, name="router", grid=(N // tm,),
        in_specs=[pl.BlockSpec((tm, ns, LANES), lambda i: (i, 0, 0)),
                  _layer_spec(jl, (D, LANES)), _layer_spec(jl, (1, LANES))],
        out_specs=[pl.BlockSpec((tm, LANES), lambda i: (i, 0)),
                   pl.BlockSpec((tm, LANES), lambda i: (i, 0)),
                   pl.BlockSpec((8, LANES), lambda i: (0, 0))],
        out_shape=[jax.ShapeDtypeStruct((N, LANES), jnp.int32), jax.ShapeDtypeStruct((N, LANES), F32),
                   jax.ShapeDtypeStruct((8, LANES), F32)],
        scratch_shapes=[pltpu.VMEM((1, LANES), F32)],
        compiler_params=_cparams(("arbitrary",)),
    )(h2, wr, br)


def _slot_kernel(idx_ref, pstart_ref, dest_ref, carry):
    i = pl.program_id(0)

    @pl.when(i == 0)
    def _():
        carry[...] = jnp.zeros_like(carry)

    tm = idx_ref.shape[0]
    idx = idx_ref[...]
    lane = lax.broadcasted_iota(jnp.int32, idx.shape, 1)
    oh1 = lane == idx[:, 0:1]
    oh2 = lane == idx[:, 1:2]
    oh = jnp.where(oh1, 1.0, jnp.where(oh2, 1.0, 0.0))
    ii = lax.broadcasted_iota(jnp.int32, (tm, tm), 0)
    jj = lax.broadcasted_iota(jnp.int32, (tm, tm), 1)
    earlier = jnp.dot(jnp.where(ii > jj, 1.0, 0.0).astype(BF16), oh.astype(BF16), preferred_element_type=F32)
    base = earlier + (carry[...] + pstart_ref[...])
    d1 = jnp.sum(jnp.where(oh1, base, 0.0), axis=-1, keepdims=True)
    d2 = jnp.sum(jnp.where(oh2, base, 0.0), axis=-1, keepdims=True)
    dest_ref[...] = jnp.where(lane == 0, d1, jnp.where(lane == 1, d2, 0.0)).astype(jnp.int32)
    carry[...] += jnp.sum(oh, axis=0, keepdims=True)


def _slots(idx, pstart, tm):
    N = idx.shape[0]
    return pl.pallas_call(
        _slot_kernel, name="slots", grid=(N // tm,),
        in_specs=[pl.BlockSpec((tm, LANES), lambda i: (i, 0)), pl.BlockSpec((1, LANES), lambda i: (0, 0))],
        out_specs=pl.BlockSpec((tm, LANES), lambda i: (i, 0)),
        out_shape=jax.ShapeDtypeStruct((N, LANES), jnp.int32),
        scratch_shapes=[pltpu.VMEM((1, LANES), F32)],
        compiler_params=_cparams(("arbitrary",)),
    )(idx, pstart)


DMA_UNROLL = 8


def _dispatch_kernel(plo_ref, phi_ref, dest_hbm, h_ref, xs_ref, dsm, zrow, isem, sem):
    tm = h_ref.shape[0]
    i = pl.program_id(0)
    cp = pltpu.make_async_copy(dest_hbm.at[i], dsm, isem)
    cp.start()

    @pl.when(i == 0)
    def _():
        zrow[...] = jnp.zeros_like(zrow)
        for e in range(N_EXPERTS):
            def zero_slot(d, c):
                pltpu.make_async_copy(zrow, xs_ref.at[d], sem).start()
                return c

            def zero_done(d, c):
                pltpu.make_async_copy(zrow, xs_ref.at[0], sem).wait()
                return c

            lax.fori_loop(plo_ref[e], phi_ref[e], zero_slot, 0)
            lax.fori_loop(plo_ref[e], phi_ref[e], zero_done, 0)

    cp.wait()

    def issue(r, c):
        pltpu.make_async_copy(h_ref.at[r], xs_ref.at[dsm[r]], sem).start()
        pltpu.make_async_copy(h_ref.at[r], xs_ref.at[dsm[tm + r]], sem).start()
        return c

    lax.fori_loop(0, tm, issue, 0, unroll=DMA_UNROLL)
    for _ in range(2):
        pltpu.make_async_copy(h_ref, xs_ref.at[pl.ds(0, tm)], sem).wait()


def _dispatch(pad_lo, pad_hi, dest_tiles, h2, rows, tm):
    N, ns, _ = h2.shape
    return pl.pallas_call(
        _dispatch_kernel, name="dispatch",
        grid_spec=pltpu.PrefetchScalarGridSpec(
            num_scalar_prefetch=2, grid=(N // tm,),
            in_specs=[pl.BlockSpec(memory_space=pl.ANY),
                      pl.BlockSpec((tm, ns, LANES), lambda i, lo, hi: (i, 0, 0))],
            out_specs=pl.BlockSpec(memory_space=pl.ANY),
            scratch_shapes=[pltpu.SMEM((2 * tm,), jnp.int32), pltpu.VMEM((ns, LANES), F32),
                            pltpu.SemaphoreType.DMA, pltpu.SemaphoreType.DMA]),
        out_shape=jax.ShapeDtypeStruct((rows, ns, LANES), F32),
        compiler_params=_cparams(("arbitrary",)),
    )(pad_lo, pad_hi, dest_tiles, h2)


def _expert_kernel(be_ref, nu_ref, xs_ref, wg_ref, wu_ref, wd_ref, ys_ref):
    del be_ref
    used = pl.program_id(0) < nu_ref[0]

    @pl.when(used)
    def _():
        xb = _from_token_tiles(xs_ref).astype(BF16)
        g = jnp.dot(xb, wg_ref[...], preferred_element_type=F32)
        u = jnp.dot(xb, wu_ref[...], preferred_element_type=F32)
        _to_token_tiles(_dot(_silu(g) * u, wd_ref[...]), ys_ref)

    @pl.when(jnp.logical_not(used))
    def _():
        ys_ref[...] = jnp.zeros_like(ys_ref)


def _experts(block_e, n_used, xs, wg, wu, wd, j):
    rows, ns, _ = xs.shape
    D = ns * LANES
    F = wg.shape[3]
    nblk = rows // MOE_ROWS

    def row_map(i, be, nu):
        return (jnp.minimum(i, nu[0] - 1), 0, 0)

    def w_map(i, be, nu):
        return (j, be[i], 0, 0)

    return pl.pallas_call(
        _expert_kernel, name="experts",
        grid_spec=pltpu.PrefetchScalarGridSpec(
            num_scalar_prefetch=2, grid=(nblk,),
            in_specs=[pl.BlockSpec((MOE_ROWS, ns, LANES), row_map),
                      pl.BlockSpec((None, None, D, F), w_map), pl.BlockSpec((None, None, D, F), w_map),
                      pl.BlockSpec((None, None, F, D), w_map)],
            out_specs=pl.BlockSpec((MOE_ROWS, ns, LANES), lambda i, be, nu: (i, 0, 0))),
        out_shape=jax.ShapeDtypeStruct((rows, ns, LANES), F32),
        compiler_params=_cparams(("arbitrary",)),
    )(block_e, n_used, xs, wg, wu, wd)


def _combine_kernel(dest_hbm, ys_hbm, gate_ref, x1_ref, g2_ref, o_ref, dsm, b0, b1, isem, sem):
    bb, tt, D = x1_ref.shape
    tm = bb * tt
    i = pl.program_id(0) * pl.num_programs(1) + pl.program_id(1)
    cp = pltpu.make_async_copy(dest_hbm.at[i], dsm, isem)
    cp.start()
    cp.wait()

    def issue(r, c):
        pltpu.make_async_copy(ys_hbm.at[dsm[r]], b0.at[r], sem).start()
        pltpu.make_async_copy(ys_hbm.at[dsm[tm + r]], b1.at[r], sem).start()
        return c

    lax.fori_loop(0, tm, issue, 0, unroll=DMA_UNROLL)
    pltpu.make_async_copy(ys_hbm.at[pl.ds(0, tm)], b0, sem).wait()
    pltpu.make_async_copy(ys_hbm.at[pl.ds(0, tm)], b1, sem).wait()
    g = gate_ref[...]
    g0 = g[:, 0:1]
    g1 = g[:, 1:2]
    for s in range(D // LANES):
        y = g0 * b0[:, s, :] + g1 * b1[:, s, :]
        lanes = slice(s * LANES, (s + 1) * LANES)
        o_ref[:, :, lanes] = x1_ref[:, :, lanes] + g2_ref[:, :, lanes] * y.reshape(bb, tt, LANES)


def _combine(dest_tiles, ys, gates, x1, mod, l):
    B, T, D = x1.shape
    bb, tt = _tok_tiles(B, T)
    nt = T // tt
    tm = bb * tt
    ns = D // LANES
    return pl.pallas_call(
        _combine_kernel, name="combine", grid=(B // bb, nt),
        in_specs=[pl.BlockSpec(memory_space=pl.ANY), pl.BlockSpec(memory_space=pl.ANY),
                  pl.BlockSpec((tm, LANES), lambda b, t: (b * nt + t, 0)),
                  pl.BlockSpec((bb, tt, D), lambda b, t: (b, t, 0)),
                  _mod_spec(l, 5, bb, D)],
        out_specs=pl.BlockSpec((bb, tt, D), lambda b, t: (b, t, 0)),
        out_shape=jax.ShapeDtypeStruct((B, T, D), F32),
        scratch_shapes=[pltpu.SMEM((2 * tm,), jnp.int32), pltpu.VMEM((tm, ns, LANES), F32),
                        pltpu.VMEM((tm, ns, LANES), F32), pltpu.SemaphoreType.DMA, pltpu.SemaphoreType.DMA],
        compiler_params=_cparams(("arbitrary", "arbitrary")),
    )(dest_tiles, ys, gates, x1, mod)


def _moe(h2, x1, mod, wr, br, wg, wu, wd, l, jl):
    B, T, D = x1.shape
    bb, tt = _tok_tiles(B, T)
    tm = bb * tt
    N = B * T
    idx, gates, cnt = _router(h2, wr, br, tm, jl)
    counts = cnt[0, :N_EXPERTS].astype(jnp.int32)
    padded = (counts + MOE_ROWS - 1) // MOE_ROWS * MOE_ROWS
    pends = jnp.cumsum(padded)
    pstart = pends - padded
    nblk = -(-(2 * N) // MOE_ROWS) + N_EXPERTS
    starts = jnp.arange(nblk, dtype=jnp.int32) * MOE_ROWS
    block_e = jnp.minimum(jnp.sum((pends[None, :] <= starts[:, None]).astype(jnp.int32), axis=1), N_EXPERTS - 1)
    n_used = (pends[-1:] // MOE_ROWS).astype(jnp.int32)
    pstart_vec = jnp.zeros((1, LANES), F32).at[0, :N_EXPERTS].set(pstart.astype(F32))
    dest = _slots(idx, pstart_vec, tm)
    dest_tiles = dest[:, :2].reshape(N // tm, tm, 2).transpose(0, 2, 1).reshape(N // tm, 2 * tm)
    xs = _dispatch(pstart + counts, pends, dest_tiles, h2, nblk * MOE_ROWS, tm)
    ys = _experts(block_e, n_used, xs, wg, wu, wd, jl)
    return _combine(dest_tiles, ys, gates, x1, mod, l)


def _final_kernel(x_ref, g_ref, o_ref):
    x = x_ref[...]
    o_ref[...] = x * lax.rsqrt(jnp.mean(x * x, axis=-1, keepdims=True) + EPS) * g_ref[...]


def _final(x, g):
    B, T, D = x.shape
    bb, tt = _tok_tiles(B, T)
    return pl.pallas_call(
        _final_kernel, name="final", grid=(B // bb, T // tt),
        in_specs=[pl.BlockSpec((bb, tt, D), lambda b, t: (b, t, 0)), pl.BlockSpec((1, D), lambda b, t: (0, 0))],
        out_specs=pl.BlockSpec((bb, tt, D), lambda b, t: (b, t, 0)),
        out_shape=jax.ShapeDtypeStruct((B, T, D), F32),
        compiler_params=_cparams(("parallel", "parallel")),
    )(x, g)


def _trunk(x, mods, st, p):
    L = mods.shape[0]
    new = [[] for _ in range(5)]
    h0_all = st[4][:, :, None, :]
    for l in range(L):
        dense = l % 2 == 0
        j = l // 2
        u, qkv, z, lru, ba = _inproj(x, mods, p['norm1'], p['w_in'], l)
        ya, n_a = _mixa(u, st[0], p['conv_a_w'], p['conv_a_b'], p['ln_a_g'], p['ln_a_b'], l)
        yb, n_q, n_d = _gdn(qkv, z, ba, st[1], st[2], p['conv_qkv_w'], p['alog'], p['dtb'], p['gdn_norm'], l)
        yc, n_cl, n_h = _lru(lru, st[3], h0_all, p['conv_lru_w'], p['conv_lru_b'], p['lru_wa'], p['lru_b_a'],
                             p['lru_wi'], p['lru_b_i'], p['lru_lambda'], p['lru_norm'], l)
        x1, h2 = _outproj(ya, yb, yc, x, mods, p['norm2'], p['w_out'], l, tile_out=not dense)
        if dense:
            x = _ffn(h2, x1, mods, p['ffn_wg'], p['ffn_wu'], p['ffn_wd'], l, j)
        else:
            x = _moe(h2, x1, mods, p['router_w'], p['router_b'], p['moe_wg'], p['moe_wu'], p['moe_wd'], l, j)
        for lst, val in zip(new, (n_a, n_q, n_d, n_cl, n_h[:, 0, :])):
            lst.append(val)
    y = _final(x, p['final_norm'])
    return y, [jnp.stack(v, axis=0) for v in new]


def kernel(x_prompt, x_sample, c_prompt, c_sample, state_conv_a, state_conv_qkv, state_delta, state_conv_lru, state_lru, w_ada, b_ada, norm1, norm2, w_in, w_out, conv_a_w, conv_a_b, ln_a_g, ln_a_b, conv_qkv_w, gdn_a_log, gdn_dt_bias, gdn_norm, conv_lru_w, conv_lru_b, lru_w_a, lru_b_a, lru_w_i, lru_b_i, lru_lambda, lru_norm, ffn_wg, ffn_wu, ffn_wd, router_w, router_b, moe_wg, moe_wu, moe_wd, final_norm):
    L, D = norm1.shape
    Bp, Bs = x_prompt.shape[0], x_sample.shape[0]
    nh = gdn_a_log.shape[1]

    def row(v):
        return v[:, None, :]

    def lanes(v):
        return jnp.zeros((L, 1, LANES), F32).at[:, 0, nh:2 * nh].set(v)

    def block_diag(w):
        n, k = w.shape[1], w.shape[2]
        eye = jnp.eye(n, dtype=w.dtype)
        return jnp.einsum('lnij,nm->lnimj', w, eye).reshape(L, n * k, n * k)

    split = 2 * 256 + 3 * 512 + 512
    w_in_r = jnp.concatenate([w_in[:, :, :split], w_in[:, :, split + 2 * nh:], w_in[:, :, split:split + 2 * nh],
                              jnp.zeros((L, D, LANES - 2 * nh), w_in.dtype)], axis=-1).astype(BF16)
    n_moe = router_w.shape[0]
    p = dict(
        norm1=row(norm1), norm2=row(norm2), w_in=w_in_r, w_out=w_out.astype(BF16),
        conv_a_w=conv_a_w, conv_a_b=row(conv_a_b), ln_a_g=row(ln_a_g), ln_a_b=row(ln_a_b),
        conv_qkv_w=conv_qkv_w, alog=lanes(gdn_a_log), dtb=lanes(gdn_dt_bias), gdn_norm=row(gdn_norm),
        conv_lru_w=conv_lru_w, conv_lru_b=row(conv_lru_b), lru_wa=block_diag(lru_w_a), lru_b_a=row(lru_b_a),
        lru_wi=block_diag(lru_w_i), lru_b_i=row(lru_b_i), lru_lambda=row(lru_lambda), lru_norm=row(lru_norm),
        ffn_wg=ffn_wg.astype(BF16), ffn_wu=ffn_wu.astype(BF16), ffn_wd=ffn_wd.astype(BF16),
        router_w=jnp.concatenate([router_w, jnp.zeros((n_moe, D, LANES - N_EXPERTS), F32)], axis=-1),
        router_b=jnp.concatenate([router_b, jnp.full((n_moe, LANES - N_EXPERTS), NEG_BIG, F32)],
                                 axis=-1)[:, None, :],
        moe_wg=moe_wg.astype(BF16), moe_wu=moe_wu.astype(BF16), moe_wd=moe_wd.astype(BF16),
        final_norm=final_norm[None, :],
    )

    c_all = jnp.concatenate([c_prompt, c_sample], axis=0)
    mod = _ada(c_all, w_ada, b_ada)
    mod = mod.reshape(L, Bp + Bs, 6, D).transpose(0, 2, 1, 3)[:, :, :, None, :]
    mod_p, mod_s = mod[:, :, :Bp], mod[:, :, Bp:]

    zero_states = [jnp.zeros((L, Bp) + s.shape[2:], F32)
                   for s in (state_conv_a, state_conv_qkv, state_delta, state_conv_lru, state_lru)]
    y_p, st_p = _trunk(x_prompt, mod_p, zero_states, p)
    y_s, st_s = _trunk(x_sample, mod_s,
                       [state_conv_a, state_conv_qkv, state_delta, state_conv_lru, state_lru], p)
    return (y_p, y_s, *st_p, *st_s)
```

```python
import functools

import jax
import jax.numpy as jnp
from jax import lax
from jax.experimental import pallas as pl
from jax.experimental.pallas import tpu as pltpu

F32 = jnp.float32
BF16 = jnp.bfloat16
EPS = 1e-6

V7X_VMEM_LIMIT_BYTES = 56 * 1024 * 1024
LANES = 128
TOKEN_TILE = 512
MOE_ROWS = 512
N_EXPERTS = 8
GDN_CHUNK = 64
LRU_C = 8.0
NEG_BIG = -1e30


def _cparams(sem):
    return pltpu.CompilerParams(dimension_semantics=sem, vmem_limit_bytes=V7X_VMEM_LIMIT_BYTES)


def _sigmoid(x):
    return jax.nn.sigmoid(x)


def _silu(x):
    return x * jax.nn.sigmoid(x)


def _softplus(x):
    return jnp.maximum(x, 0.0) + jnp.log1p(jnp.exp(-jnp.abs(x)))


def _dot(a, b):
    return jnp.dot(a.astype(BF16), b.astype(BF16), preferred_element_type=F32)


def _dot_nt(a, b):
    return lax.dot_general(a.astype(BF16), b.astype(BF16), (((1,), (1,)), ((), ())),
                           preferred_element_type=F32)


def _split2(x):
    hi = x.astype(BF16)
    lo = (x - hi.astype(F32)).astype(BF16)
    return hi, lo


def _split3(x):
    hi = x.astype(BF16)
    r = x - hi.astype(F32)
    mid = r.astype(BF16)
    lo = (r - mid.astype(F32)).astype(BF16)
    return hi, mid, lo


def _dot_hp(a, b):
    ah, al = _split2(a)
    bh, bl = _split2(b)
    d = functools.partial(jnp.dot, preferred_element_type=F32)
    return d(ah, bh) + (d(ah, bl) + d(al, bh))


def _dot_exact_lhs(a_bf16, b):
    d = functools.partial(jnp.dot, preferred_element_type=F32)
    h, m, l = _split3(b)
    return d(a_bf16, h) + (d(a_bf16, m) + d(a_bf16, l))


def _tok_tiles(B, T, target=TOKEN_TILE):
    if T >= target:
        assert T % target == 0
        return 1, target
    bb = max(1, min(B, target // T))
    assert B % bb == 0
    return bb, T


def _ada_kernel(c_ref, w_ref, b_ref, o_ref):
    c = c_ref[...]
    o_ref[...] = _dot(_silu(c), w_ref[...]) + b_ref[...]


def _ada(c_all, w_ada, b_ada):
    L, D, D6 = w_ada.shape
    Bc = c_all.shape[0]
    tn = 1536
    return pl.pallas_call(
        _ada_kernel, name="ada", grid=(L, D6 // tn),
        in_specs=[pl.BlockSpec((Bc, D), lambda l, j: (0, 0)),
                  pl.BlockSpec((None, D, tn), lambda l, j: (l, 0, j)),
                  pl.BlockSpec((None, 1, tn), lambda l, j: (l, 0, j))],
        out_specs=pl.BlockSpec((None, Bc, tn), lambda l, j: (l, 0, j)),
        out_shape=jax.ShapeDtypeStruct((L, Bc, D6), F32),
        compiler_params=_cparams(("parallel", "parallel")),
    )(c_all, w_ada, b_ada.reshape(L, 1, D6))


def _mod_spec(l, j, bb, D):
    return pl.BlockSpec((None, None, bb, 1, D), lambda b, *_: (l, j, b, 0, 0))


def _layer_spec(l, shape):
    zeros = (0,) * len(shape)
    return pl.BlockSpec((None,) + tuple(shape), lambda *_: (l,) + zeros)


def _state_spec(l, bb, shape):
    zeros = (0,) * len(shape)
    return pl.BlockSpec((None, bb) + tuple(shape), lambda b, *_: (l, b) + zeros)


_IN_AG = (0, 512)
_IN_QKV = (512, 2048)
_IN_Z = (2048, 2560)
_IN_LRU = (2560, 3072)
_IN_BA = (3072, 3200)
_IN_COLS = 3200


def _inproj_kernel(x_ref, sc_ref, sh_ref, g_ref, w_ref, u_ref, qkv_ref, z_ref, lru_ref, ba_ref):
    bb, tt, D = x_ref.shape
    x = x_ref[...]
    h = x * lax.rsqrt(jnp.mean(x * x, axis=-1, keepdims=True) + EPS) * g_ref[...]
    h = h * (1.0 + sc_ref[...]) + sh_ref[...]
    hb = h.reshape(bb * tt, D).astype(BF16)

    def seg(lo_hi):
        return jnp.dot(hb, w_ref[:, lo_hi[0]:lo_hi[1]], preferred_element_type=F32)

    ag = seg(_IN_AG)
    u_ref[...] = (ag[:, :256] * _sigmoid(ag[:, 256:])).reshape(bb, tt, 256)
    qkv_ref[...] = seg(_IN_QKV).reshape(bb, tt, 1536)
    z_ref[...] = seg(_IN_Z).reshape(bb, tt, 512)
    lru_ref[...] = seg(_IN_LRU).reshape(bb, tt, 512)
    ba_ref[...] = seg(_IN_BA).reshape(bb, tt, LANES)


def _inproj(x, mod, gain, w, l):
    B, T, D = x.shape
    bb, tt = _tok_tiles(B, T)
    widths = (256, 1536, 512, 512, LANES)
    return pl.pallas_call(
        _inproj_kernel, name="inproj", grid=(B // bb, T // tt),
        in_specs=[pl.BlockSpec((bb, tt, D), lambda b, t: (b, t, 0)),
                  _mod_spec(l, 1, bb, D), _mod_spec(l, 0, bb, D),
                  _layer_spec(l, (1, D)), _layer_spec(l, (D, _IN_COLS))],
        out_specs=[pl.BlockSpec((bb, tt, n), lambda b, t: (b, t, 0)) for n in widths],
        out_shape=[jax.ShapeDtypeStruct((B, T, n), F32) for n in widths],
        compiler_params=_cparams(("parallel", "parallel")),
    )(x, mod, mod, gain, w)


def _mixa_kernel(u_ref, buf_ref, w_ref, cb_ref, lg_ref, lb_ref, y_ref, ns_ref, xp_ref):
    bb, tc, ch = u_ref.shape
    width = w_ref.shape[0]
    t = pl.program_id(1)

    @pl.when(t == 0)
    def _():
        xp_ref[:, 2:32, :] = buf_ref[...]

    @pl.when(t > 0)
    def _():
        xp_ref[:, 0:32, :] = xp_ref[:, tc:tc + 32, :]

    xp_ref[:, 32:32 + tc, :] = u_ref[...]
    ns_ref[...] = xp_ref[:, tc + 2:tc + 32, :]

    rs = min(tc, 64)
    win = rs + 32
    for b in range(bb):
        for r0 in range(0, tc, rs):
            w0 = xp_ref[b, r0:r0 + win, :]
            shifted = [w0] + [pltpu.roll(w0, win - r, 0) for r in range(1, 8)]
            acc = jnp.zeros((rs, ch), F32) + cb_ref[...]
            for j in range(width):
                a, r = divmod(2 + j, 8)
                acc = acc + w_ref[j:j + 1, :] * shifted[r][8 * a:8 * a + rs, :]
            mu = jnp.mean(acc, axis=-1, keepdims=True)
            d = acc - mu
            var = jnp.mean(d * d, axis=-1, keepdims=True)
            yn = d * lax.rsqrt(var + EPS) * lg_ref[...] + lb_ref[...]
            y_ref[b, r0:r0 + rs, :] = _silu(yn)


def _mixa(u, buf, w, cb, lg, lb, l):
    B, T, ch = u.shape
    if T >= 256:
        bb, tc = 1, 256
    else:
        bb, tc = min(B, 8), T
    assert T % tc == 0 and B % bb == 0
    nb = buf.shape[2]
    vec = _layer_spec(l, (1, ch))
    return pl.pallas_call(
        _mixa_kernel, name="mixa", grid=(B // bb, T // tc),
        in_specs=[pl.BlockSpec((bb, tc, ch), lambda b, t: (b, t, 0)),
                  _state_spec(l, bb, (nb, ch)),
                  _layer_spec(l, w.shape[1:]), vec, vec, vec],
        out_specs=[pl.BlockSpec((bb, tc, ch), lambda b, t: (b, t, 0)),
                   pl.BlockSpec((bb, nb, ch), lambda b, t: (b, 0, 0))],
        out_shape=[jax.ShapeDtypeStruct((B, T, ch), F32), jax.ShapeDtypeStruct((B, nb, ch), F32)],
        scratch_shapes=[pltpu.VMEM((bb, 32 + tc, ch), F32)],
        compiler_params=_cparams(("parallel", "arbitrary")),
    )(u, buf, w, cb, lg, lb)


def _gdn_kernel(qkv_ref, z_ref, ba_ref, sq_ref, sd_ref, cw_ref, alog_ref, dtb_ref, gn_ref,
                y_ref, nq_ref, nd_ref, xp_ref, qn_ref, bg_ref, s_ref, *, C):
    bb, tt, width = qkv_ref.shape
    nh = sd_ref.shape[1]
    dk = sd_ref.shape[2]
    ntap = cw_ref.shape[0]
    t = pl.program_id(1)

    @pl.when(t == 0)
    def _():
        xp_ref[:, 5:8, :] = sq_ref[...]
        s_ref[...] = sd_ref[...]

    @pl.when(t > 0)
    def _():
        xp_ref[:, 5:8, :] = xp_ref[:, tt + 5:tt + 8, :]

    xp_ref[:, 8:8 + tt, :] = qkv_ref[...]
    nq_ref[...] = xp_ref[:, tt + 5:tt + 8, :]

    rs = min(tt, 128)
    for sl in range(width // dk):
        lo = sl * dk
        for r0 in range(0, tt, rs):
            acc = jnp.zeros((bb, rs, dk), F32)
            for j in range(ntap):
                acc = acc + cw_ref[j:j + 1, lo:lo + dk] * xp_ref[:, r0 + 5 + j:r0 + 5 + j + rs, lo:lo + dk]
            s = _silu(acc)
            if sl < 2 * nh:
                s = s * lax.rsqrt(jnp.sum(s * s, axis=-1, keepdims=True) + EPS)
            if sl < nh:
                s = s * (dk ** -0.5)
            qn_ref[:, r0:r0 + rs, lo:lo + dk] = s

    ba = ba_ref[...]
    lane3 = lax.broadcasted_iota(jnp.int32, ba.shape, 2)
    gdec = -jnp.exp(alog_ref[...]) * _softplus(ba + dtb_ref[...])
    bg_ref[...] = jnp.where(lane3 < nh, _sigmoid(ba), jnp.where(lane3 < 2 * nh, gdec, 0.0))

    ii = lax.broadcasted_iota(jnp.int32, (C, C), 0)
    jj = lax.broadcasted_iota(jnp.int32, (C, C), 1)
    tri_incl = jnp.where(ii >= jj, 1.0, 0.0).astype(BF16)
    lane_c = lax.broadcasted_iota(jnp.int32, (C, LANES), 1)
    nsteps = C.bit_length() - 1
    assert (1 << nsteps) == C
    ident = jnp.where(ii == jj, 1.0, 0.0)
    pair_mask = []
    for s in range(nsteps):
        same_big = jnp.right_shift(ii, s + 1) == jnp.right_shift(jj, s + 1)
        diff_small = jnp.right_shift(ii, s) != jnp.right_shift(jj, s)
        pair_mask.append((same_big, diff_small))

    def pair_part(s, m):
        return jnp.where(pair_mask[s][0], jnp.where(pair_mask[s][1], m, 0.0), 0.0)

    nc = tt // C
    probs = [(b, c, h) for b in range(bb) for c in range(nc) for h in range(nh)]
    bgc, gcs, gct = {}, {}, {}
    for b in range(bb):
        for c in range(nc):
            bgc[b, c] = bg_ref[b, c * C:(c + 1) * C, :]
            gcs[b, c] = _dot_exact_lhs(tri_incl, jnp.where(lane_c < nh, 0.0, bgc[b, c]))
    for key in gcs:
        gct[key] = gcs[key].T

    def rows(b, c, slab):
        return qn_ref[b, c * C:(c + 1) * C, slab * dk:(slab + 1) * dk]

    q = {p: rows(p[0], p[1], p[2]) for p in probs}
    k = {p: rows(p[0], p[1], nh + p[2]) for p in probs}
    gcol = {(b, c, h): gcs[b, c][:, nh + h:nh + h + 1] for b, c, h in probs}
    kb = {(b, c, h): k[b, c, h] * bgc[b, c][:, h:h + 1] for b, c, h in probs}
    a2 = {p: _dot_nt(jnp.concatenate([kb[p], q[p]], axis=0), k[p]) for p in probs}
    decay = {(b, c, h): jnp.exp(jnp.where(ii >= jj, gcol[b, c, h] - gct[b, c][nh + h:nh + h + 1, :], -jnp.inf))
             for b, c, h in probs}
    low = {p: jnp.where(ii > jj, a2[p][:C] * decay[p], 0.0) for p in probs}
    tinv = {p: ident - pair_part(0, low[p]) for p in probs}
    for lvl in range(1, nsteps):
        half = {p: _dot(tinv[p], pair_part(lvl, low[p])) for p in probs}
        tinv = {p: tinv[p] - _dot(half[p], tinv[p]) for p in probs}
    rhs = {(b, c, h): jnp.concatenate([rows(b, c, 2 * nh + h) * bgc[b, c][:, h:h + 1],
                                        kb[b, c, h] * jnp.exp(gcol[b, c, h])], axis=1) for b, c, h in probs}
    uw = {p: _dot(tinv[p], rhs[p]) for p in probs}

    state = {(b, h): s_ref[b, h] for b in range(bb) for h in range(nh)}
    for c in range(nc):
        cur = [(b, c, h) for b in range(bb) for h in range(nh)]
        wq = {p: _dot(jnp.concatenate([uw[p][:, dk:], q[p] * jnp.exp(gcol[p])], axis=0), state[p[0], p[2]])
              for p in cur}
        v_new = {p: uw[p][:, :dk] - wq[p][:C] for p in cur}
        o = {p: wq[p][C:] + _dot(a2[p][C:] * decay[p], v_new[p]) for p in cur}
        for b, _, h in cur:
            p = (b, c, h)
            gl = gcs[b, c][C - 1:C, nh + h:nh + h + 1]
            kdec = k[p] * jnp.exp(gl - gcol[p])
            upd = lax.dot_general(kdec.astype(BF16), v_new[p].astype(BF16), (((0,), (0,)), ((), ())),
                                  preferred_element_type=F32)
            state[b, h] = state[b, h] * jnp.exp(gl) + upd
        for b, _, h in cur:
            p = (b, c, h)
            on = o[p] * lax.rsqrt(jnp.mean(o[p] * o[p], axis=-1, keepdims=True) + EPS) * gn_ref[...]
            zz = z_ref[b, c * C:(c + 1) * C, h * dk:(h + 1) * dk]
            y_ref[b, c * C:(c + 1) * C, h * dk:(h + 1) * dk] = on * _silu(zz)
    for (b, h), val in state.items():
        s_ref[b, h] = val
    nd_ref[...] = s_ref[...]


def _gdn(qkv, z, ba, s_qkv, s_delta, cw, alog_vec, dtb_vec, gn, l):
    B, T, width = qkv.shape
    nh, dk = s_delta.shape[2], s_delta.shape[3]
    C = min(GDN_CHUNK, T)
    if T > C:
        bb = min(B, 2)
        tt = min(T, 8 * C // bb)
    else:
        bb, tt = min(B, 8), T
    assert T % tt == 0 and tt % C == 0 and B % bb == 0
    vec = _layer_spec(l, (1, LANES))
    kern = functools.partial(_gdn_kernel, C=C)
    return pl.pallas_call(
        kern, name="gdn", grid=(B // bb, T // tt),
        in_specs=[pl.BlockSpec((bb, tt, width), lambda b, t: (b, t, 0)),
                  pl.BlockSpec((bb, tt, nh * dk), lambda b, t: (b, t, 0)),
                  pl.BlockSpec((bb, tt, LANES), lambda b, t: (b, t, 0)),
                  _state_spec(l, bb, (3, width)),
                  _state_spec(l, bb, (nh, dk, dk)),
                  _layer_spec(l, cw.shape[1:]), vec, vec, vec],
        out_specs=[pl.BlockSpec((bb, tt, nh * dk), lambda b, t: (b, t, 0)),
                   pl.BlockSpec((bb, 3, width), lambda b, t: (b, 0, 0)),
                   pl.BlockSpec((bb, nh, dk, dk), lambda b, t: (b, 0, 0, 0))],
        out_shape=[jax.ShapeDtypeStruct((B, T, nh * dk), F32),
                   jax.ShapeDtypeStruct((B, 3, width), F32),
                   jax.ShapeDtypeStruct((B, nh, dk, dk), F32)],
        scratch_shapes=[pltpu.VMEM((bb, 8 + tt, width), F32), pltpu.VMEM((bb, tt, width), F32),
                        pltpu.VMEM((bb, tt, LANES), F32), pltpu.VMEM((bb, nh, dk, dk), F32)],
        compiler_params=_cparams(("parallel", "arbitrary")),
    )(qkv, z, ba, s_qkv, s_delta, cw, alog_vec, dtb_vec, gn)


def _lru_kernel(lru_ref, cs_ref, h0_ref, cw_ref, cb_ref, wa_ref, ba_ref, wi_ref, bi_ref, lam_ref, ng_ref,
                y_ref, ncs_ref, nh_ref, xp_ref, sa_ref, sb_ref, h_ref):
    bb, tc, two_ch = lru_ref.shape
    ch = two_ch // 2
    ntap = cw_ref.shape[0]
    t = pl.program_id(1)

    @pl.when(t == 0)
    def _():
        xp_ref[:, 5:8, :] = cs_ref[...]
        h_ref[...] = h0_ref[...]

    @pl.when(t > 0)
    def _():
        xp_ref[:, 5:8, :] = xp_ref[:, tc + 5:tc + 8, :]

    xp_ref[:, 8:8 + tc, :] = lru_ref[:, :, 0:ch]
    ncs_ref[...] = xp_ref[:, tc + 5:tc + 8, :]

    xc = jnp.zeros((bb, tc, ch), F32) + cb_ref[...]
    for j in range(ntap):
        xc = xc + cw_ref[j:j + 1, :] * xp_ref[:, 5 + j:5 + j + tc, :]
    x2 = xc.reshape(bb * tc, ch)
    r = _sigmoid(_dot_hp(x2, wa_ref[...]) + ba_ref[...])
    i = _sigmoid(_dot_hp(x2, wi_ref[...]) + bi_ref[...])
    log_a = (-LRU_C * r) * _softplus(-lam_ref[...])
    a = jnp.exp(log_a)
    bv = jnp.sqrt(1.0 - jnp.exp(2.0 * log_a)) * (i * x2)
    a3 = a.reshape(bb, tc, ch)
    b3 = bv.reshape(bb, tc, ch)

    sa_ref[:, 0:tc, :] = jnp.ones((bb, tc, ch), F32)
    sb_ref[:, 0:tc, :] = jnp.zeros((bb, tc, ch), F32)
    d = 1
    while d < tc:
        sa_ref[:, tc:2 * tc, :] = a3
        sb_ref[:, tc:2 * tc, :] = b3
        ap = sa_ref[:, tc - d:2 * tc - d, :]
        bp = sb_ref[:, tc - d:2 * tc - d, :]
        b3 = a3 * bp + b3
        a3 = a3 * ap
        d *= 2
    h = a3 * h_ref[...] + b3
    h_last = h[:, tc - 1:tc, :]
    h_ref[...] = h_last
    nh_ref[...] = h_last

    y = h * jax.nn.gelu(lru_ref[:, :, ch:two_ch])
    y_ref[...] = y * lax.rsqrt(jnp.mean(y * y, axis=-1, keepdims=True) + EPS) * ng_ref[...]


def _lru(lru, cs, h0, cw, cb, wa, ba, wi, bi, lam, ng, l):
    B, T, two_ch = lru.shape
    ch = two_ch // 2
    if T >= 256:
        bb, tc = 1, 256
    else:
        bb, tc = min(B, 16), T
    assert T % tc == 0 and B % bb == 0
    vec = _layer_spec(l, (1, ch))
    mat = _layer_spec(l, (ch, ch))
    return pl.pallas_call(
        _lru_kernel, name="lru", grid=(B // bb, T // tc),
        in_specs=[pl.BlockSpec((bb, tc, two_ch), lambda b, t: (b, t, 0)),
                  _state_spec(l, bb, (3, ch)),
                  _state_spec(l, bb, (1, ch)),
                  _layer_spec(l, cw.shape[1:]), vec, mat, vec, mat, vec, vec, vec],
        out_specs=[pl.BlockSpec((bb, tc, ch), lambda b, t: (b, t, 0)),
                   pl.BlockSpec((bb, 3, ch), lambda b, t: (b, 0, 0)),
                   pl.BlockSpec((bb, 1, ch), lambda b, t: (b, 0, 0))],
        out_shape=[jax.ShapeDtypeStruct((B, T, ch), F32), jax.ShapeDtypeStruct((B, 3, ch), F32),
                   jax.ShapeDtypeStruct((B, 1, ch), F32)],
        scratch_shapes=[pltpu.VMEM((bb, 8 + tc, ch), F32), pltpu.VMEM((bb, 2 * tc, ch), F32),
                        pltpu.VMEM((bb, 2 * tc, ch), F32), pltpu.VMEM((bb, 1, ch), F32)],
        compiler_params=_cparams(("parallel", "arbitrary")),
    )(lru, cs, h0, cw, cb, wa, ba, wi, bi, lam, ng)


def _outproj_kernel(ya_ref, yb_ref, yc_ref, x_ref, g1_ref, sc_ref, sh_ref, n2_ref, w_ref, x1_ref, h2_ref):
    bb, tt, D = x_ref.shape
    m = bb * tt
    na, nb = ya_ref.shape[2], yb_ref.shape[2]
    mix = (_dot(ya_ref[...].reshape(m, na), w_ref[0:na, :])
           + _dot(yb_ref[...].reshape(m, nb), w_ref[na:na + nb, :])
           + _dot(yc_ref[...].reshape(m, D - na - nb), w_ref[na + nb:D, :]))
    x1 = x_ref[...] + g1_ref[...] * mix.reshape(bb, tt, D)
    x1_ref[...] = x1
    h = x1 * lax.rsqrt(jnp.mean(x1 * x1, axis=-1, keepdims=True) + EPS) * n2_ref[...]
    h = (h * (1.0 + sc_ref[...]) + sh_ref[...]).reshape(m, D)
    if len(h2_ref.shape) == 2:
        h2_ref[...] = h
    else:
        _to_token_tiles(h, h2_ref)


def _to_token_tiles(val, ref):
    for s in range(ref.shape[1]):
        ref[:, s, :] = val[:, s * LANES:(s + 1) * LANES]


def _from_token_tiles(ref):
    return jnp.concatenate([ref[:, s, :] for s in range(ref.shape[1])], axis=1)


def _outproj(ya, yb, yc, x, mod, gain2, w, l, tile_out):
    B, T, D = x.shape
    bb, tt = _tok_tiles(B, T)
    nt = T // tt

    def tok(n):
        return pl.BlockSpec((bb, tt, n), lambda b, t: (b, t, 0))

    if tile_out:
        h2_spec = pl.BlockSpec((bb * tt, D // LANES, LANES), lambda b, t: (b * nt + t, 0, 0))
        h2_shape = jax.ShapeDtypeStruct((B * T, D // LANES, LANES), F32)
    else:
        h2_spec = pl.BlockSpec((bb * tt, D), lambda b, t: (b * nt + t, 0))
        h2_shape = jax.ShapeDtypeStruct((B * T, D), F32)
    return pl.pallas_call(
        _outproj_kernel, name="outproj", grid=(B // bb, nt),
        in_specs=[tok(ya.shape[2]), tok(yb.shape[2]), tok(yc.shape[2]), tok(D),
                  _mod_spec(l, 2, bb, D), _mod_spec(l, 4, bb, D), _mod_spec(l, 3, bb, D),
                  _layer_spec(l, (1, D)), _layer_spec(l, (D, D))],
        out_specs=[tok(D), h2_spec],
        out_shape=[jax.ShapeDtypeStruct((B, T, D), F32), h2_shape],
        compiler_params=_cparams(("parallel", "parallel")),
    )(ya, yb, yc, x, mod, mod, mod, gain2, w)


def _ffn_kernel(h_ref, x1_ref, g2_ref, wg_ref, wu_ref, wd_ref, o_ref, acc_ref):
    bb, tt, D = x1_ref.shape
    j = pl.program_id(2)

    @pl.when(j == 0)
    def _():
        acc_ref[...] = jnp.zeros_like(acc_ref)

    hb = h_ref[...].astype(BF16)
    g = jnp.dot(hb, wg_ref[...], preferred_element_type=F32)
    u = jnp.dot(hb, wu_ref[...], preferred_element_type=F32)
    acc_ref[...] += _dot(_silu(g) * u, wd_ref[...])

    @pl.when(j == pl.num_programs(2) - 1)
    def _():
        o_ref[...] = x1_ref[...] + g2_ref[...] * acc_ref[...].reshape(bb, tt, D)


def _ffn(h2, x1, mod, wg, wu, wd, l, jl):
    B, T, D = x1.shape
    bb, tt = _tok_tiles(B, T)
    nt = T // tt
    F = wg.shape[2]
    tf = F // 2
    assert tf % LANES == 0
    return pl.pallas_call(
        _ffn_kernel, name="ffn", grid=(B // bb, nt, F // tf),
        in_specs=[pl.BlockSpec((bb * tt, D), lambda b, t, j: (b * nt + t, 0)),
                  pl.BlockSpec((bb, tt, D), lambda b, t, j: (b, t, 0)),
                  _mod_spec(l, 5, bb, D),
                  pl.BlockSpec((None, D, tf), lambda b, t, j: (jl, 0, j)),
                  pl.BlockSpec((None, D, tf), lambda b, t, j: (jl, 0, j)),
                  pl.BlockSpec((None, tf, D), lambda b, t, j: (jl, j, 0))],
        out_specs=pl.BlockSpec((bb, tt, D), lambda b, t, j: (b, t, 0)),
        out_shape=jax.ShapeDtypeStruct((B, T, D), F32),
        scratch_shapes=[pltpu.VMEM((bb * tt, D), F32)],
        compiler_params=_cparams(("parallel", "parallel", "arbitrary")),
    )(h2, x1, mod, wg, wu, wd)


def _router_kernel(h_ref, wr_ref, br_ref, idx_ref, gate_ref, cnt_ref, cnt_acc):
    i = pl.program_id(0)

    @pl.when(i == 0)
    def _():
        cnt_acc[...] = jnp.zeros_like(cnt_acc)

    logits = _dot_hp(_from_token_tiles(h_ref), wr_ref[...]) + br_ref[...]
    lane = lax.broadcasted_iota(jnp.int32, logits.shape, 1).astype(F32)
    m1 = jnp.max(logits, axis=-1, keepdims=True)
    i1 = jnp.min(jnp.where(logits == m1, lane, float(LANES)), axis=-1, keepdims=True)
    rest = jnp.where(lane == i1, NEG_BIG, logits)
    m2 = jnp.max(rest, axis=-1, keepdims=True)
    i2 = jnp.min(jnp.where(rest == m2, lane, float(LANES)), axis=-1, keepdims=True)
    e = jnp.exp(m2 - m1)
    g1 = 1.0 / (1.0 + e)
    g2 = e / (1.0 + e)
    idx_ref[...] = jnp.where(lane == 0.0, i1, jnp.where(lane == 1.0, i2, 0.0)).astype(jnp.int32)
    gate_ref[...] = jnp.where(lane == 0.0, g1, jnp.where(lane == 1.0, g2, 0.0))
    oh = jnp.where(lane == i1, 1.0, jnp.where(lane == i2, 1.0, 0.0))
    cnt_acc[...] += jnp.sum(oh, axis=0, keepdims=True)
    cnt_ref[...] = jnp.broadcast_to(cnt_acc[...], cnt_ref.shape)


def _router(h2, wr, br, tm, jl):
    N, ns, _ = h2.shape
    D = ns * LANES
    return pl.pallas_call(
        _router_kernel, name="router", grid=(N // tm,),
        in_specs=[pl.BlockSpec((tm, ns, LANES), lambda i: (i, 0, 0)),
                  _layer_spec(jl, (D, LANES)), _layer_spec(jl, (1, LANES))],
        out_specs=[pl.BlockSpec((tm, LANES), lambda i: (i, 0)),
                   pl.BlockSpec((tm, LANES), lambda i: (i, 0)),
                   pl.BlockSpec((8, LANES), lambda i: (0, 0))],
        out_shape=[jax.ShapeDtypeStruct((N, LANES), jnp.int32), jax.ShapeDtypeStruct((N, LANES), F32),
                   jax.ShapeDtypeStruct((8, LANES), F32)],
        scratch_shapes=[pltpu.VMEM((1, LANES), F32)],
        compiler_params=_cparams(("arbitrary",)),
    )(h2, wr, br)


def _slot_kernel(idx_ref, pstart_ref, dest_ref, carry):
    i = pl.program_id(0)

    @pl.when(i == 0)
    def _():
        carry[...] = jnp.zeros_like(carry)

    tm = idx_ref.shape[0]
    idx = idx_ref[...]
    lane = lax.broadcasted_iota(jnp.int32, idx.shape, 1)
    oh1 = lane == idx[:, 0:1]
    oh2 = lane == idx[:, 1:2]
    oh = jnp.where(oh1, 1.0, jnp.where(oh2, 1.0, 0.0))
    ii = lax.broadcasted_iota(jnp.int32, (tm, tm), 0)
    jj = lax.broadcasted_iota(jnp.int32, (tm, tm), 1)
    earlier = jnp.dot(jnp.where(ii > jj, 1.0, 0.0).astype(BF16), oh.astype(BF16), preferred_element_type=F32)
    base = earlier + (carry[...] + pstart_ref[...])
    d1 = jnp.sum(jnp.where(oh1, base, 0.0), axis=-1, keepdims=True)
    d2 = jnp.sum(jnp.where(oh2, base, 0.0), axis=-1, keepdims=True)
    dest_ref[...] = jnp.where(lane == 0, d1, jnp.where(lane == 1, d2, 0.0)).astype(jnp.int32)
    carry[...] += jnp.sum(oh, axis=0, keepdims=True)


def _slots(idx, pstart, tm):
    N = idx.shape[0]
    return pl.pallas_call(
        _slot_kernel, name="slots", grid=(N // tm,),
        in_specs=[pl.BlockSpec((tm, LANES), lambda i: (i, 0)), pl.BlockSpec((1, LANES), lambda i: (0, 0))],
        out_specs=pl.BlockSpec((tm, LANES), lambda i: (i, 0)),
        out_shape=jax.ShapeDtypeStruct((N, LANES), jnp.int32),
        scratch_shapes=[pltpu.VMEM((1, LANES), F32)],
        compiler_params=_cparams(("arbitrary",)),
    )(idx, pstart)


DMA_UNROLL = 8


def _dispatch_kernel(plo_ref, phi_ref, dest_hbm, h_ref, xs_ref, dsm, zrow, isem, sem):
    tm = h_ref.shape[0]
    i = pl.program_id(0)
    cp = pltpu.make_async_copy(dest_hbm.at[i], dsm, isem)
    cp.start()

    @pl.when(i == 0)
    def _():
        zrow[...] = jnp.zeros_like(zrow)
        for e in range(plo_ref.shape[0]):
            def zero_slot(d, c):
                pltpu.make_async_copy(zrow, xs_ref.at[d], sem).start()
                return c

            def zero_done(d, c):
                pltpu.make_async_copy(zrow, xs_ref.at[0], sem).wait()
                return c

            lax.fori_loop(plo_ref[e], phi_ref[e], zero_slot, 0)
            lax.fori_loop(plo_ref[e], phi_ref[e], zero_done, 0)

    cp.wait()

    def issue(r, c):
        pltpu.make_async_copy(h_ref.at[r], xs_ref.at[dsm[r]], sem).start(priority=0)
        pltpu.make_async_copy(h_ref.at[r], xs_ref.at[dsm[tm + r]], sem).start(priority=1)
        return c

    lax.fori_loop(0, tm, issue, 0, unroll=DMA_UNROLL)
    for _ in range(2):
        pltpu.make_async_copy(h_ref, xs_ref.at[pl.ds(0, tm)], sem).wait()


def _dispatch(pad_lo, pad_hi, dest_tiles, h2, rows, tm):
    N, ns, _ = h2.shape
    return pl.pallas_call(
        _dispatch_kernel, name="dispatch",
        grid_spec=pltpu.PrefetchScalarGridSpec(
            num_scalar_prefetch=2, grid=(N // tm,),
            in_specs=[pl.BlockSpec(memory_space=pl.ANY),
                      pl.BlockSpec((tm, ns, LANES), lambda i, lo, hi: (i, 0, 0))],
            out_specs=pl.BlockSpec(memory_space=pl.ANY),
            scratch_shapes=[pltpu.SMEM((2 * tm,), jnp.int32), pltpu.VMEM((ns, LANES), F32),
                            pltpu.SemaphoreType.DMA, pltpu.SemaphoreType.DMA]),
        out_shape=jax.ShapeDtypeStruct((rows, ns, LANES), F32),
        compiler_params=_cparams(("arbitrary",)),
    )(pad_lo, pad_hi, dest_tiles, h2)


def _expert_kernel(be_ref, nu_ref, xs_ref, wg_ref, wu_ref, wd_ref, ys_ref):
    del be_ref
    used = pl.program_id(0) < nu_ref[0]

    @pl.when(used)
    def _():
        xb = _from_token_tiles(xs_ref).astype(BF16)
        g = jnp.dot(xb, wg_ref[...], preferred_element_type=F32)
        u = jnp.dot(xb, wu_ref[...], preferred_element_type=F32)
        _to_token_tiles(_dot(_silu(g) * u, wd_ref[...]), ys_ref)

    @pl.when(jnp.logical_not(used))
    def _():
        ys_ref[...] = jnp.zeros_like(ys_ref)


def _experts(block_e, n_used, xs, wg, wu, wd, j):
    rows, ns, _ = xs.shape
    D = ns * LANES
    F = wg.shape[3]
    nblk = rows // MOE_ROWS

    def row_map(i, be, nu):
        return (jnp.minimum(i, nu[0] - 1), 0, 0)

    def w_map(i, be, nu):
        return (j, be[i], 0, 0)

    return pl.pallas_call(
        _expert_kernel, name="experts",
        grid_spec=pltpu.PrefetchScalarGridSpec(
            num_scalar_prefetch=2, grid=(nblk,),
            in_specs=[pl.BlockSpec((MOE_ROWS, ns, LANES), row_map),
                      pl.BlockSpec((None, None, D, F), w_map), pl.BlockSpec((None, None, D, F), w_map),
                      pl.BlockSpec((None, None, F, D), w_map)],
            out_specs=pl.BlockSpec((MOE_ROWS, ns, LANES), lambda i, be, nu: (i, 0, 0))),
        out_shape=jax.ShapeDtypeStruct((rows, ns, LANES), F32),
        compiler_params=_cparams(("arbitrary",)),
    )(block_e, n_used, xs, wg, wu, wd)


def _combine_kernel(dest_hbm, ys_hbm, gate_ref, x1_ref, g2_ref, o_ref, dsm, b0, b1, isem, sem):
    bb, tt, D = x1_ref.shape
    tm = bb * tt
    i = pl.program_id(0) * pl.num_programs(1) + pl.program_id(1)
    cp = pltpu.make_async_copy(dest_hbm.at[i], dsm, isem)
    cp.start()
    cp.wait()

    def issue(r, c):
        pltpu.make_async_copy(ys_hbm.at[dsm[r]], b0.at[r], sem).start(priority=0)
        pltpu.make_async_copy(ys_hbm.at[dsm[tm + r]], b1.at[r], sem).start(priority=1)
        return c

    lax.fori_loop(0, tm, issue, 0, unroll=DMA_UNROLL)
    pltpu.make_async_copy(ys_hbm.at[pl.ds(0, tm)], b0, sem).wait()
    pltpu.make_async_copy(ys_hbm.at[pl.ds(0, tm)], b1, sem).wait()
    g = gate_ref[...]
    g0 = g[:, 0:1]
    g1 = g[:, 1:2]
    for s in range(D // LANES):
        y = g0 * b0[:, s, :] + g1 * b1[:, s, :]
        lanes = slice(s * LANES, (s + 1) * LANES)
        o_ref[:, :, lanes] = x1_ref[:, :, lanes] + g2_ref[:, :, lanes] * y.reshape(bb, tt, LANES)


def _combine(dest_tiles, ys, gates, x1, mod, l):
    B, T, D = x1.shape
    bb, tt = _tok_tiles(B, T)
    nt = T // tt
    tm = bb * tt
    ns = D // LANES
    return pl.pallas_call(
        _combine_kernel, name="combine", grid=(B // bb, nt),
        in_specs=[pl.BlockSpec(memory_space=pl.ANY), pl.BlockSpec(memory_space=pl.ANY),
                  pl.BlockSpec((tm, LANES), lambda b, t: (b * nt + t, 0)),
                  pl.BlockSpec((bb, tt, D), lambda b, t: (b, t, 0)),
                  _mod_spec(l, 5, bb, D)],
        out_specs=pl.BlockSpec((bb, tt, D), lambda b, t: (b, t, 0)),
        out_shape=jax.ShapeDtypeStruct((B, T, D), F32),
        scratch_shapes=[pltpu.SMEM((2 * tm,), jnp.int32), pltpu.VMEM((tm, ns, LANES), F32),
                        pltpu.VMEM((tm, ns, LANES), F32), pltpu.SemaphoreType.DMA, pltpu.SemaphoreType.DMA],
        compiler_params=_cparams(("arbitrary", "arbitrary")),
    )(dest_tiles, ys, gates, x1, mod)


def _moe(h2, x1, mod, wr, br, wg, wu, wd, l, jl):
    B, T, D = x1.shape
    bb, tt = _tok_tiles(B, T)
    tm = bb * tt
    N = B * T
    idx, gates, cnt = _router(h2, wr, br, tm, jl)
    counts = cnt[0, :N_EXPERTS].astype(jnp.int32)
    padded = (counts + MOE_ROWS - 1) // MOE_ROWS * MOE_ROWS
    pends = jnp.cumsum(padded)
    pstart = pends - padded
    nblk = -(-(2 * N) // MOE_ROWS) + N_EXPERTS
    starts = jnp.arange(nblk, dtype=jnp.int32) * MOE_ROWS
    block_e = jnp.minimum(jnp.sum((pends[None, :] <= starts[:, None]).astype(jnp.int32), axis=1), N_EXPERTS - 1)
    n_used = (pends[-1:] // MOE_ROWS).astype(jnp.int32)
    pstart_vec = jnp.zeros((1, LANES), F32).at[0, :N_EXPERTS].set(pstart.astype(F32))
    dest = _slots(idx, pstart_vec, tm)
    dest_tiles = dest[:, :2].reshape(N // tm, tm, 2).transpose(0, 2, 1).reshape(N // tm, 2 * tm)
    rows = nblk * MOE_ROWS
    pad_lo = jnp.concatenate([pstart + counts, pends[-1:]])
    pad_hi = jnp.concatenate([pends, jnp.full((1,), rows, jnp.int32)])
    xs = _dispatch(pad_lo, pad_hi, dest_tiles, h2, rows, tm)
    ys = _experts(block_e, n_used, xs, wg, wu, wd, jl)
    return _combine(dest_tiles, ys, gates, x1, mod, l)


def _final_kernel(x_ref, g_ref, o_ref):
    x = x_ref[...]
    o_ref[...] = x * lax.rsqrt(jnp.mean(x * x, axis=-1, keepdims=True) + EPS) * g_ref[...]


def _final(x, g):
    B, T, D = x.shape
    bb, tt = _tok_tiles(B, T)
    return pl.pallas_call(
        _final_kernel, name="final", grid=(B // bb, T // tt),
        in_specs=[pl.BlockSpec((bb, tt, D), lambda b, t: (b, t, 0)), pl.BlockSpec((1, D), lambda b, t: (0, 0))],
        out_specs=pl.BlockSpec((bb, tt, D), lambda b, t: (b, t, 0)),
        out_shape=jax.ShapeDtypeStruct((B, T, D), F32),
        compiler_params=_cparams(("parallel", "parallel")),
    )(x, g)


def _trunk(x, mods, st, p):
    L = mods.shape[0]
    new = [[] for _ in range(5)]
    h0_all = st[4][:, :, None, :]
    for l in range(L):
        dense = l % 2 == 0
        j = l // 2
        u, qkv, z, lru, ba = _inproj(x, mods, p['norm1'], p['w_in'], l)
        ya, n_a = _mixa(u, st[0], p['conv_a_w'], p['conv_a_b'], p['ln_a_g'], p['ln_a_b'], l)
        yb, n_q, n_d = _gdn(qkv, z, ba, st[1], st[2], p['conv_qkv_w'], p['alog'], p['dtb'], p['gdn_norm'], l)
        yc, n_cl, n_h = _lru(lru, st[3], h0_all, p['conv_lru_w'], p['conv_lru_b'], p['lru_wa'], p['lru_b_a'],
                             p['lru_wi'], p['lru_b_i'], p['lru_lambda'], p['lru_norm'], l)
        x1, h2 = _outproj(ya, yb, yc, x, mods, p['norm2'], p['w_out'], l, tile_out=not dense)
        if dense:
            x = _ffn(h2, x1, mods, p['ffn_wg'], p['ffn_wu'], p['ffn_wd'], l, j)
        else:
            x = _moe(h2, x1, mods, p['router_w'], p['router_b'], p['moe_wg'], p['moe_wu'], p['moe_wd'], l, j)
        for lst, val in zip(new, (n_a, n_q, n_d, n_cl, n_h[:, 0, :])):
            lst.append(val)
    y = _final(x, p['final_norm'])
    return y, [jnp.stack(v, axis=0) for v in new]


def kernel(x_prompt, x_sample, c_prompt, c_sample, state_conv_a, state_conv_qkv, state_delta, state_conv_lru, state_lru, w_ada, b_ada, norm1, norm2, w_in, w_out, conv_a_w, conv_a_b, ln_a_g, ln_a_b, conv_qkv_w, gdn_a_log, gdn_dt_bias, gdn_norm, conv_lru_w, conv_lru_b, lru_w_a, lru_b_a, lru_w_i, lru_b_i, lru_lambda, lru_norm, ffn_wg, ffn_wu, ffn_wd, router_w, router_b, moe_wg, moe_wu, moe_wd, final_norm):
    L, D = norm1.shape
    Bp, Bs = x_prompt.shape[0], x_sample.shape[0]
    nh = gdn_a_log.shape[1]

    def row(v):
        return v[:, None, :]

    def lanes(v):
        return jnp.zeros((L, 1, LANES), F32).at[:, 0, nh:2 * nh].set(v)

    def block_diag(w):
        n, k = w.shape[1], w.shape[2]
        eye = jnp.eye(n, dtype=w.dtype)
        return jnp.einsum('lnij,nm->lnimj', w, eye).reshape(L, n * k, n * k)

    split = 2 * 256 + 3 * 512 + 512
    w_in_r = jnp.concatenate([w_in[:, :, :split], w_in[:, :, split + 2 * nh:], w_in[:, :, split:split + 2 * nh],
                              jnp.zeros((L, D, LANES - 2 * nh), w_in.dtype)], axis=-1).astype(BF16)
    n_moe = router_w.shape[0]
    p = dict(
        norm1=row(norm1), norm2=row(norm2), w_in=w_in_r, w_out=w_out.astype(BF16),
        conv_a_w=conv_a_w, conv_a_b=row(conv_a_b), ln_a_g=row(ln_a_g), ln_a_b=row(ln_a_b),
        conv_qkv_w=conv_qkv_w, alog=lanes(gdn_a_log), dtb=lanes(gdn_dt_bias), gdn_norm=row(gdn_norm),
        conv_lru_w=conv_lru_w, conv_lru_b=row(conv_lru_b), lru_wa=block_diag(lru_w_a), lru_b_a=row(lru_b_a),
        lru_wi=block_diag(lru_w_i), lru_b_i=row(lru_b_i), lru_lambda=row(lru_lambda), lru_norm=row(lru_norm),
        ffn_wg=ffn_wg.astype(BF16), ffn_wu=ffn_wu.astype(BF16), ffn_wd=ffn_wd.astype(BF16),
        router_w=jnp.concatenate([router_w, jnp.zeros((n_moe, D, LANES - N_EXPERTS), F32)], axis=-1),
        router_b=jnp.concatenate([router_b, jnp.full((n_moe, LANES - N_EXPERTS), NEG_BIG, F32)],
                                 axis=-1)[:, None, :],
        moe_wg=moe_wg.astype(BF16), moe_wu=moe_wu.astype(BF16), moe_wd=moe_wd.astype(BF16),
        final_norm=final_norm[None, :],
    )

    c_all = jnp.concatenate([c_prompt, c_sample], axis=0)
    mod = _ada(c_all, w_ada, b_ada)
    mod = mod.reshape(L, Bp + Bs, 6, D).transpose(0, 2, 1, 3)[:, :, :, None, :]
    mod_p, mod_s = mod[:, :, :Bp], mod[:, :, Bp:]

    zero_states = [jnp.zeros((L, Bp) + s.shape[2:], F32)
                   for s in (state_conv_a, state_conv_qkv, state_delta, state_conv_lru, state_lru)]
    y_p, st_p = _trunk(x_prompt, mod_p, zero_states, p)
    y_s, st_s = _trunk(x_sample, mod_s,
                       [state_conv_a, state_conv_qkv, state_delta, state_conv_lru, state_lru], p)
    return (y_p, y_s, *st_p, *st_s)
```

```python
import functools

import jax
import jax.numpy as jnp
from jax import lax
from jax.experimental import pallas as pl
from jax.experimental.pallas import tpu as pltpu

F32 = jnp.float32
BF16 = jnp.bfloat16
EPS = 1e-6

V7X_VMEM_LIMIT_BYTES = 56 * 1024 * 1024
LANES = 128
TOKEN_TILE = 512
MOE_ROWS = 512
N_EXPERTS = 8
GDN_CHUNK = 64
LRU_C = 8.0
NEG_BIG = -1e30


def _cparams(sem):
    return pltpu.CompilerParams(dimension_semantics=sem, vmem_limit_bytes=V7X_VMEM_LIMIT_BYTES)


def _sigmoid(x):
    return jax.nn.sigmoid(x)


def _silu(x):
    return x * jax.nn.sigmoid(x)


def _softplus(x):
    return jnp.maximum(x, 0.0) + jnp.log1p(jnp.exp(-jnp.abs(x)))


def _dot(a, b):
    return jnp.dot(a.astype(BF16), b.astype(BF16), preferred_element_type=F32)


def _dot_nt(a, b):
    return lax.dot_general(a.astype(BF16), b.astype(BF16), (((1,), (1,)), ((), ())),
                           preferred_element_type=F32)


def _split2(x):
    hi = x.astype(BF16)
    lo = (x - hi.astype(F32)).astype(BF16)
    return hi, lo


def _split3(x):
    hi = x.astype(BF16)
    r = x - hi.astype(F32)
    mid = r.astype(BF16)
    lo = (r - mid.astype(F32)).astype(BF16)
    return hi, mid, lo


def _dot_hp(a, b):
    ah, al = _split2(a)
    bh, bl = _split2(b)
    d = functools.partial(jnp.dot, preferred_element_type=F32)
    return d(ah, bh) + (d(ah, bl) + d(al, bh))


def _dot_exact_lhs(a_bf16, b):
    d = functools.partial(jnp.dot, preferred_element_type=F32)
    h, m, l = _split3(b)
    return d(a_bf16, h) + (d(a_bf16, m) + d(a_bf16, l))


def _tok_tiles(B, T, target=TOKEN_TILE):
    if T >= target:
        assert T % target == 0
        return 1, target
    bb = max(1, min(B, target // T))
    assert B % bb == 0
    return bb, T


def _ada_kernel(c_ref, w_ref, b_ref, o_ref):
    c = c_ref[...]
    o_ref[...] = _dot(_silu(c), w_ref[...]) + b_ref[...]


def _ada(c_all, w_ada, b_ada):
    L, D, D6 = w_ada.shape
    Bc = c_all.shape[0]
    tn = 1536
    return pl.pallas_call(
        _ada_kernel, name="ada", grid=(L, D6 // tn),
        in_specs=[pl.BlockSpec((Bc, D), lambda l, j: (0, 0)),
                  pl.BlockSpec((None, D, tn), lambda l, j: (l, 0, j)),
                  pl.BlockSpec((None, 1, tn), lambda l, j: (l, 0, j))],
        out_specs=pl.BlockSpec((None, Bc, tn), lambda l, j: (l, 0, j)),
        out_shape=jax.ShapeDtypeStruct((L, Bc, D6), F32),
        compiler_params=_cparams(("parallel", "parallel")),
    )(c_all, w_ada, b_ada.reshape(L, 1, D6))


def _mod_spec(l, j, bb, D):
    return pl.BlockSpec((None, None, bb, 1, D), lambda b, *_: (l, j, b, 0, 0))


def _layer_spec(l, shape):
    zeros = (0,) * len(shape)
    return pl.BlockSpec((None,) + tuple(shape), lambda *_: (l,) + zeros)


def _state_spec(l, bb, shape):
    zeros = (0,) * len(shape)
    return pl.BlockSpec((None, bb) + tuple(shape), lambda b, *_: (l, b) + zeros)


_IN_AG = (0, 512)
_IN_QKV = (512, 2048)
_IN_Z = (2048, 2560)
_IN_HEAD = 2560
_TAIL_LRU = (0, 512)
_TAIL_BA = (512, 640)
_IN_TAIL = 640


def _inproj_kernel(x_ref, sc_ref, sh_ref, g_ref, w_ref, wt_ref, u_ref, qkv_ref, z_ref, lru_ref, ba_ref):
    bb, tt, D = x_ref.shape
    x = x_ref[...]
    h = x * lax.rsqrt(jnp.mean(x * x, axis=-1, keepdims=True) + EPS) * g_ref[...]
    h = h * (1.0 + sc_ref[...]) + sh_ref[...]
    hb = h.reshape(bb * tt, D).astype(BF16)

    def seg(lo_hi, ref=w_ref):
        return jnp.dot(hb, ref[:, lo_hi[0]:lo_hi[1]], preferred_element_type=F32)

    ag = seg(_IN_AG)
    u_ref[...] = (ag[:, :256] * _sigmoid(ag[:, 256:])).reshape(bb, tt, 256)
    qkv_ref[...] = seg(_IN_QKV).reshape(bb, tt, 1536)
    z_ref[...] = seg(_IN_Z).reshape(bb, tt, 512)
    lru_ref[...] = seg(_TAIL_LRU, wt_ref).reshape(bb, tt, 512)
    ba_ref[...] = seg(_TAIL_BA, wt_ref).reshape(bb, tt, LANES)


def _inproj(x, mod, gain, w, wt, l):
    B, T, D = x.shape
    bb, tt = _tok_tiles(B, T)
    widths = (256, 1536, 512, 512, LANES)
    return pl.pallas_call(
        _inproj_kernel, name="inproj", grid=(B // bb, T // tt),
        in_specs=[pl.BlockSpec((bb, tt, D), lambda b, t: (b, t, 0)),
                  _mod_spec(l, 1, bb, D), _mod_spec(l, 0, bb, D),
                  _layer_spec(l, (1, D)), _layer_spec(l, (D, _IN_HEAD)), _layer_spec(l, (D, _IN_TAIL))],
        out_specs=[pl.BlockSpec((bb, tt, n), lambda b, t: (b, t, 0)) for n in widths],
        out_shape=[jax.ShapeDtypeStruct((B, T, n), F32) for n in widths],
        compiler_params=_cparams(("parallel", "parallel")),
    )(x, mod, mod, gain, w, wt)


def _mixa_kernel(u_ref, buf_ref, w_ref, cb_ref, lg_ref, lb_ref, y_ref, ns_ref, xp_ref):
    bb, tc, ch = u_ref.shape
    width = w_ref.shape[0]
    t = pl.program_id(1)

    @pl.when(t == 0)
    def _():
        xp_ref[:, 2:32, :] = buf_ref[...]

    @pl.when(t > 0)
    def _():
        xp_ref[:, 0:32, :] = xp_ref[:, tc:tc + 32, :]

    xp_ref[:, 32:32 + tc, :] = u_ref[...]
    ns_ref[...] = xp_ref[:, tc + 2:tc + 32, :]

    rs = min(tc, 64)
    win = rs + 32
    for b in range(bb):
        for r0 in range(0, tc, rs):
            w0 = xp_ref[b, r0:r0 + win, :]
            shifted = [w0] + [pltpu.roll(w0, win - r, 0) for r in range(1, 8)]
            acc = jnp.zeros((rs, ch), F32) + cb_ref[...]
            for j in range(width):
                a, r = divmod(2 + j, 8)
                acc = acc + w_ref[j:j + 1, :] * shifted[r][8 * a:8 * a + rs, :]
            mu = jnp.mean(acc, axis=-1, keepdims=True)
            d = acc - mu
            var = jnp.mean(d * d, axis=-1, keepdims=True)
            yn = d * lax.rsqrt(var + EPS) * lg_ref[...] + lb_ref[...]
            y_ref[b, r0:r0 + rs, :] = _silu(yn)


def _mixa(u, buf, w, cb, lg, lb, l):
    B, T, ch = u.shape
    if T >= 512:
        bb, tc = 1, 512
    elif T >= 256:
        bb, tc = 1, 256
    else:
        bb, tc = min(B, 8), T
    assert T % tc == 0 and B % bb == 0
    nb = buf.shape[2]
    vec = _layer_spec(l, (1, ch))
    return pl.pallas_call(
        _mixa_kernel, name="mixa", grid=(B // bb, T // tc),
        in_specs=[pl.BlockSpec((bb, tc, ch), lambda b, t: (b, t, 0)),
                  _state_spec(l, bb, (nb, ch)),
                  _layer_spec(l, w.shape[1:]), vec, vec, vec],
        out_specs=[pl.BlockSpec((bb, tc, ch), lambda b, t: (b, t, 0)),
                   pl.BlockSpec((bb, nb, ch), lambda b, t: (b, 0, 0))],
        out_shape=[jax.ShapeDtypeStruct((B, T, ch), F32), jax.ShapeDtypeStruct((B, nb, ch), F32)],
        scratch_shapes=[pltpu.VMEM((bb, 32 + tc, ch), F32)],
        compiler_params=_cparams(("parallel", "arbitrary")),
    )(u, buf, w, cb, lg, lb)


def _gdn_kernel(qkv_ref, z_ref, ba_ref, sq_ref, sd_ref, cw_ref, alog_ref, dtb_ref, gn_ref,
                y_ref, nq_ref, nd_ref, xp_ref, qn_ref, bg_ref, s_ref, *, C):
    bb, tt, width = qkv_ref.shape
    nh = sd_ref.shape[1]
    dk = sd_ref.shape[2]
    ntap = cw_ref.shape[0]
    t = pl.program_id(1)

    @pl.when(t == 0)
    def _():
        xp_ref[:, 5:8, :] = sq_ref[...]
        s_ref[...] = sd_ref[...]

    @pl.when(t > 0)
    def _():
        xp_ref[:, 5:8, :] = xp_ref[:, tt + 5:tt + 8, :]

    xp_ref[:, 8:8 + tt, :] = qkv_ref[...]
    nq_ref[...] = xp_ref[:, tt + 5:tt + 8, :]

    rs = min(tt, 128)
    win = rs + 8
    for b in range(bb):
        for sl in range(width // dk):
            lo = sl * dk
            for r0 in range(0, tt, rs):
                w0 = xp_ref[b, r0:r0 + win, lo:lo + dk]
                acc = cw_ref[ntap - 1:ntap, lo:lo + dk] * w0[8:8 + rs, :]
                for j in range(ntap - 1):
                    acc = acc + cw_ref[j:j + 1, lo:lo + dk] * pltpu.roll(w0, win - 5 - j, 0)[:rs, :]
                s = _silu(acc)
                if sl < 2 * nh:
                    s = s * lax.rsqrt(jnp.sum(s * s, axis=-1, keepdims=True) + EPS)
                if sl < nh:
                    s = s * (dk ** -0.5)
                qn_ref[b, r0:r0 + rs, lo:lo + dk] = s

    ba = ba_ref[...]
    lane3 = lax.broadcasted_iota(jnp.int32, ba.shape, 2)
    gdec = -jnp.exp(alog_ref[...]) * _softplus(ba + dtb_ref[...])
    bg_ref[...] = jnp.where(lane3 < nh, _sigmoid(ba), jnp.where(lane3 < 2 * nh, gdec, 0.0))

    ii = lax.broadcasted_iota(jnp.int32, (C, C), 0)
    jj = lax.broadcasted_iota(jnp.int32, (C, C), 1)
    tri_incl = jnp.where(ii >= jj, 1.0, 0.0).astype(BF16)
    lane_c = lax.broadcasted_iota(jnp.int32, (C, LANES), 1)
    nsteps = C.bit_length() - 1
    assert (1 << nsteps) == C
    ident = jnp.where(ii == jj, 1.0, 0.0)
    pair_mask = []
    for s in range(nsteps):
        same_big = jnp.right_shift(ii, s + 1) == jnp.right_shift(jj, s + 1)
        diff_small = jnp.right_shift(ii, s) != jnp.right_shift(jj, s)
        pair_mask.append((same_big, diff_small))

    def pair_part(s, m):
        return jnp.where(pair_mask[s][0], jnp.where(pair_mask[s][1], m, 0.0), 0.0)

    nc = tt // C
    probs = [(b, c, h) for b in range(bb) for c in range(nc) for h in range(nh)]
    bgc, gcs, gct = {}, {}, {}
    for b in range(bb):
        for c in range(nc):
            bgc[b, c] = bg_ref[b, c * C:(c + 1) * C, :]
            gcs[b, c] = _dot_exact_lhs(tri_incl, jnp.where(lane_c < nh, 0.0, bgc[b, c]))
    for key in gcs:
        gct[key] = gcs[key].T

    def rows(b, c, slab):
        return qn_ref[b, c * C:(c + 1) * C, slab * dk:(slab + 1) * dk]

    q = {p: rows(p[0], p[1], p[2]) for p in probs}
    k = {p: rows(p[0], p[1], nh + p[2]) for p in probs}
    gcol = {(b, c, h): gcs[b, c][:, nh + h:nh + h + 1] for b, c, h in probs}
    kb = {(b, c, h): k[b, c, h] * bgc[b, c][:, h:h + 1] for b, c, h in probs}
    a2 = {p: _dot_nt(jnp.concatenate([kb[p], q[p]], axis=0), k[p]) for p in probs}
    decay = {(b, c, h): jnp.exp(jnp.where(ii >= jj, gcol[b, c, h] - gct[b, c][nh + h:nh + h + 1, :], -jnp.inf))
             for b, c, h in probs}
    low = {p: jnp.where(ii > jj, a2[p][:C] * decay[p], 0.0) for p in probs}
    tinv = {p: ident - pair_part(0, low[p]) for p in probs}
    for lvl in range(1, nsteps):
        half = {p: _dot(tinv[p], pair_part(lvl, low[p])) for p in probs}
        tinv = {p: tinv[p] - _dot(half[p], tinv[p]) for p in probs}
    rhs = {(b, c, h): jnp.concatenate([rows(b, c, 2 * nh + h) * bgc[b, c][:, h:h + 1],
                                        kb[b, c, h] * jnp.exp(gcol[b, c, h])], axis=1) for b, c, h in probs}
    uw = {p: _dot(tinv[p], rhs[p]) for p in probs}

    state = {(b, h): s_ref[b, h] for b in range(bb) for h in range(nh)}
    for c in range(nc):
        cur = [(b, c, h) for b in range(bb) for h in range(nh)]
        wq = {p: _dot(jnp.concatenate([uw[p][:, dk:], q[p] * jnp.exp(gcol[p])], axis=0), state[p[0], p[2]])
              for p in cur}
        v_new = {p: uw[p][:, :dk] - wq[p][:C] for p in cur}
        o = {p: wq[p][C:] + _dot(a2[p][C:] * decay[p], v_new[p]) for p in cur}
        for b, _, h in cur:
            p = (b, c, h)
            gl = gcs[b, c][C - 1:C, nh + h:nh + h + 1]
            kdec = k[p] * jnp.exp(gl - gcol[p])
            upd = lax.dot_general(kdec.astype(BF16), v_new[p].astype(BF16), (((0,), (0,)), ((), ())),
                                  preferred_element_type=F32)
            state[b, h] = state[b, h] * jnp.exp(gl) + upd
        for b, _, h in cur:
            p = (b, c, h)
            on = o[p] * lax.rsqrt(jnp.mean(o[p] * o[p], axis=-1, keepdims=True) + EPS) * gn_ref[...]
            zz = z_ref[b, c * C:(c + 1) * C, h * dk:(h + 1) * dk]
            y_ref[b, c * C:(c + 1) * C, h * dk:(h + 1) * dk] = on * _silu(zz)
    for (b, h), val in state.items():
        s_ref[b, h] = val
    nd_ref[...] = s_ref[...]


def _gdn(qkv, z, ba, s_qkv, s_delta, cw, alog_vec, dtb_vec, gn, l):
    B, T, width = qkv.shape
    nh, dk = s_delta.shape[2], s_delta.shape[3]
    C = min(GDN_CHUNK, T)
    if T > C:
        bb = min(B, 2)
        tt = min(T, 8 * C // bb)
    else:
        bb, tt = min(B, 8), T
    assert T % tt == 0 and tt % C == 0 and B % bb == 0
    vec = _layer_spec(l, (1, LANES))
    kern = functools.partial(_gdn_kernel, C=C)
    return pl.pallas_call(
        kern, name="gdn", grid=(B // bb, T // tt),
        in_specs=[pl.BlockSpec((bb, tt, width), lambda b, t: (b, t, 0)),
                  pl.BlockSpec((bb, tt, nh * dk), lambda b, t: (b, t, 0)),
                  pl.BlockSpec((bb, tt, LANES), lambda b, t: (b, t, 0)),
                  _state_spec(l, bb, (3, width)),
                  _state_spec(l, bb, (nh, dk, dk)),
                  _layer_spec(l, cw.shape[1:]), vec, vec, vec],
        out_specs=[pl.BlockSpec((bb, tt, nh * dk), lambda b, t: (b, t, 0)),
                   pl.BlockSpec((bb, 3, width), lambda b, t: (b, 0, 0)),
                   pl.BlockSpec((bb, nh, dk, dk), lambda b, t: (b, 0, 0, 0))],
        out_shape=[jax.ShapeDtypeStruct((B, T, nh * dk), F32),
                   jax.ShapeDtypeStruct((B, 3, width), F32),
                   jax.ShapeDtypeStruct((B, nh, dk, dk), F32)],
        scratch_shapes=[pltpu.VMEM((bb, 8 + tt, width), F32), pltpu.VMEM((bb, tt, width), F32),
                        pltpu.VMEM((bb, tt, LANES), F32), pltpu.VMEM((bb, nh, dk, dk), F32)],
        compiler_params=_cparams(("parallel", "arbitrary")),
    )(qkv, z, ba, s_qkv, s_delta, cw, alog_vec, dtb_vec, gn)


def _lru_kernel(lru_ref, cs_ref, h0_ref, cw_ref, cb_ref, wa_ref, ba_ref, wi_ref, bi_ref, lam_ref, ng_ref,
                y_ref, ncs_ref, nh_ref, xp_ref, sa_ref, sb_ref, h_ref):
    bb, tc, two_ch = lru_ref.shape
    ch = two_ch // 2
    ntap = cw_ref.shape[0]
    t = pl.program_id(1)

    @pl.when(t == 0)
    def _():
        xp_ref[:, 5:8, :] = cs_ref[...]
        h_ref[...] = h0_ref[...]

    @pl.when(t > 0)
    def _():
        xp_ref[:, 5:8, :] = xp_ref[:, tc + 5:tc + 8, :]

    xp_ref[:, 8:8 + tc, :] = lru_ref[:, :, 0:ch]
    ncs_ref[...] = xp_ref[:, tc + 5:tc + 8, :]

    xc = jnp.zeros((bb, tc, ch), F32) + cb_ref[...]
    for j in range(ntap):
        xc = xc + cw_ref[j:j + 1, :] * xp_ref[:, 5 + j:5 + j + tc, :]
    x2 = xc.reshape(bb * tc, ch)
    r = _sigmoid(_dot_hp(x2, wa_ref[...]) + ba_ref[...])
    i = _sigmoid(_dot_hp(x2, wi_ref[...]) + bi_ref[...])
    log_a = (-LRU_C * r) * _softplus(-lam_ref[...])
    a = jnp.exp(log_a)
    bv = jnp.sqrt(1.0 - jnp.exp(2.0 * log_a)) * (i * x2)
    a3 = a.reshape(bb, tc, ch)
    b3 = bv.reshape(bb, tc, ch)

    sa_ref[:, 0:tc, :] = jnp.ones((bb, tc, ch), F32)
    sb_ref[:, 0:tc, :] = jnp.zeros((bb, tc, ch), F32)
    d = 1
    while d < tc:
        sa_ref[:, tc:2 * tc, :] = a3
        sb_ref[:, tc:2 * tc, :] = b3
        ap = sa_ref[:, tc - d:2 * tc - d, :]
        bp = sb_ref[:, tc - d:2 * tc - d, :]
        b3 = a3 * bp + b3
        a3 = a3 * ap
        d *= 2
    h = a3 * h_ref[...] + b3
    h_last = h[:, tc - 1:tc, :]
    h_ref[...] = h_last
    nh_ref[...] = h_last

    y = h * jax.nn.gelu(lru_ref[:, :, ch:two_ch])
    y_ref[...] = y * lax.rsqrt(jnp.mean(y * y, axis=-1, keepdims=True) + EPS) * ng_ref[...]


def _lru(lru, cs, h0, cw, cb, wa, ba, wi, bi, lam, ng, l):
    B, T, two_ch = lru.shape
    ch = two_ch // 2
    if T >= 512:
        bb, tc = 1, 512
    elif T >= 256:
        bb, tc = 1, 256
    else:
        bb, tc = min(B, 16), T
    assert T % tc == 0 and B % bb == 0
    vec = _layer_spec(l, (1, ch))
    mat = _layer_spec(l, (ch, ch))
    return pl.pallas_call(
        _lru_kernel, name="lru", grid=(B // bb, T // tc),
        in_specs=[pl.BlockSpec((bb, tc, two_ch), lambda b, t: (b, t, 0)),
                  _state_spec(l, bb, (3, ch)),
                  _state_spec(l, bb, (1, ch)),
                  _layer_spec(l, cw.shape[1:]), vec, mat, vec, mat, vec, vec, vec],
        out_specs=[pl.BlockSpec((bb, tc, ch), lambda b, t: (b, t, 0)),
                   pl.BlockSpec((bb, 3, ch), lambda b, t: (b, 0, 0)),
                   pl.BlockSpec((bb, 1, ch), lambda b, t: (b, 0, 0))],
        out_shape=[jax.ShapeDtypeStruct((B, T, ch), F32), jax.ShapeDtypeStruct((B, 3, ch), F32),
                   jax.ShapeDtypeStruct((B, 1, ch), F32)],
        scratch_shapes=[pltpu.VMEM((bb, 8 + tc, ch), F32), pltpu.VMEM((bb, 2 * tc, ch), F32),
                        pltpu.VMEM((bb, 2 * tc, ch), F32), pltpu.VMEM((bb, 1, ch), F32)],
        compiler_params=_cparams(("parallel", "arbitrary")),
    )(lru, cs, h0, cw, cb, wa, ba, wi, bi, lam, ng)


def _outproj_kernel(ya_ref, yb_ref, yc_ref, x_ref, g1_ref, sc_ref, sh_ref, n2_ref, w_ref, x1_ref, h2_ref):
    bb, tt, D = x_ref.shape
    m = bb * tt
    na, nb = ya_ref.shape[2], yb_ref.shape[2]
    mix = (_dot(ya_ref[...].reshape(m, na), w_ref[0:na, :])
           + _dot(yb_ref[...].reshape(m, nb), w_ref[na:na + nb, :])
           + _dot(yc_ref[...].reshape(m, D - na - nb), w_ref[na + nb:D, :]))
    x1 = x_ref[...] + g1_ref[...] * mix.reshape(bb, tt, D)
    x1_ref[...] = x1
    h = x1 * lax.rsqrt(jnp.mean(x1 * x1, axis=-1, keepdims=True) + EPS) * n2_ref[...]
    h = (h * (1.0 + sc_ref[...]) + sh_ref[...]).reshape(m, D)
    if len(h2_ref.shape) == 2:
        h2_ref[...] = h
    else:
        _to_token_tiles(h, h2_ref)


def _to_token_tiles(val, ref):
    for s in range(ref.shape[1]):
        ref[:, s, :] = val[:, s * LANES:(s + 1) * LANES]


def _from_token_tiles(ref):
    return jnp.concatenate([ref[:, s, :] for s in range(ref.shape[1])], axis=1)


def _outproj(ya, yb, yc, x, mod, gain2, w, l, tile_out):
    B, T, D = x.shape
    bb, tt = _tok_tiles(B, T)
    nt = T // tt

    def tok(n):
        return pl.BlockSpec((bb, tt, n), lambda b, t: (b, t, 0))

    if tile_out:
        h2_spec = pl.BlockSpec((bb * tt, D // LANES, LANES), lambda b, t: (b * nt + t, 0, 0))
        h2_shape = jax.ShapeDtypeStruct((B * T, D // LANES, LANES), F32)
    else:
        h2_spec = pl.BlockSpec((bb * tt, D), lambda b, t: (b * nt + t, 0))
        h2_shape = jax.ShapeDtypeStruct((B * T, D), F32)
    return pl.pallas_call(
        _outproj_kernel, name="outproj", grid=(B // bb, nt),
        in_specs=[tok(ya.shape[2]), tok(yb.shape[2]), tok(yc.shape[2]), tok(D),
                  _mod_spec(l, 2, bb, D), _mod_spec(l, 4, bb, D), _mod_spec(l, 3, bb, D),
                  _layer_spec(l, (1, D)), _layer_spec(l, (D, D))],
        out_specs=[tok(D), h2_spec],
        out_shape=[jax.ShapeDtypeStruct((B, T, D), F32), h2_shape],
        compiler_params=_cparams(("parallel", "parallel")),
    )(ya, yb, yc, x, mod, mod, mod, gain2, w)


def _ffn_kernel(h_ref, x1_ref, g2_ref, wg_ref, wu_ref, wd_ref, o_ref, acc_ref):
    bb, tt, D = x1_ref.shape
    j = pl.program_id(2)

    @pl.when(j == 0)
    def _():
        acc_ref[...] = jnp.zeros_like(acc_ref)

    hb = h_ref[...].astype(BF16)
    g = jnp.dot(hb, wg_ref[...], preferred_element_type=F32)
    u = jnp.dot(hb, wu_ref[...], preferred_element_type=F32)
    acc_ref[...] += _dot(_silu(g) * u, wd_ref[...])

    @pl.when(j == pl.num_programs(2) - 1)
    def _():
        o_ref[...] = x1_ref[...] + g2_ref[...] * acc_ref[...].reshape(bb, tt, D)


def _ffn(h2, x1, mod, wg, wu, wd, l, jl):
    B, T, D = x1.shape
    bb, tt = _tok_tiles(B, T)
    nt = T // tt
    F = wg.shape[2]
    tf = F // 2
    assert tf % LANES == 0
    return pl.pallas_call(
        _ffn_kernel, name="ffn", grid=(B // bb, nt, F // tf),
        in_specs=[pl.BlockSpec((bb * tt, D), lambda b, t, j: (b * nt + t, 0)),
                  pl.BlockSpec((bb, tt, D), lambda b, t, j: (b, t, 0)),
                  _mod_spec(l, 5, bb, D),
                  pl.BlockSpec((None, D, tf), lambda b, t, j: (jl, 0, j)),
                  pl.BlockSpec((None, D, tf), lambda b, t, j: (jl, 0, j)),
                  pl.BlockSpec((None, tf, D), lambda b, t, j: (jl, j, 0))],
        out_specs=pl.BlockSpec((bb, tt, D), lambda b, t, j: (b, t, 0)),
        out_shape=jax.ShapeDtypeStruct((B, T, D), F32),
        scratch_shapes=[pltpu.VMEM((bb * tt, D), F32)],
        compiler_params=_cparams(("parallel", "parallel", "arbitrary")),
    )(h2, x1, mod, wg, wu, wd)


def _router_kernel(h_ref, wr_ref, br_ref, idx_ref, gate_ref, cnt_ref, cnt_acc):
    i = pl.program_id(0)

    @pl.when(i == 0)
    def _():
        cnt_acc[...] = jnp.zeros_like(cnt_acc)

    logits = _dot_hp(_from_token_tiles(h_ref), wr_ref[...]) + br_ref[...]
    lane = lax.broadcasted_iota(jnp.int32, logits.shape, 1).astype(F32)
    m1 = jnp.max(logits, axis=-1, keepdims=True)
    i1 = jnp.min(jnp.where(logits == m1, lane, float(LANES)), axis=-1, keepdims=True)
    rest = jnp.where(lane == i1, NEG_BIG, logits)
    m2 = jnp.max(rest, axis=-1, keepdims=True)
    i2 = jnp.min(jnp.where(rest == m2, lane, float(LANES)), axis=-1, keepdims=True)
    e = jnp.exp(m2 - m1)
    g1 = 1.0 / (1.0 + e)
    g2 = e / (1.0 + e)
    idx_ref[...] = jnp.where(lane == 0.0, i1, jnp.where(lane == 1.0, i2, 0.0)).astype(jnp.int32)
    gate_ref[...] = jnp.where(lane == 0.0, g1, jnp.where(lane == 1.0, g2, 0.0))
    oh = jnp.where(lane == i1, 1.0, jnp.where(lane == i2, 1.0, 0.0))
    cnt_acc[...] += jnp.sum(oh, axis=0, keepdims=True)
    cnt_ref[...] = jnp.broadcast_to(cnt_acc[...], cnt_ref.shape)


def _router(h2, wr, br, tm, jl):
    N, ns, _ = h2.shape
    D = ns * LANES
    return pl.pallas_call(
        _router_kernel, name="router", grid=(N // tm,),
        in_specs=[pl.BlockSpec((tm, ns, LANES), lambda i: (i, 0, 0)),
                  _layer_spec(jl, (D, LANES)), _layer_spec(jl, (1, LANES))],
        out_specs=[pl.BlockSpec((tm, LANES), lambda i: (i, 0)),
                   pl.BlockSpec((tm, LANES), lambda i: (i, 0)),
                   pl.BlockSpec((8, LANES), lambda i: (0, 0))],
        out_shape=[jax.ShapeDtypeStruct((N, LANES), jnp.int32), jax.ShapeDtypeStruct((N, LANES), F32),
                   jax.ShapeDtypeStruct((8, LANES), F32)],
        scratch_shapes=[pltpu.VMEM((1, LANES), F32)],
        compiler_params=_cparams(("arbitrary",)),
    )(h2, wr, br)


def _slot_kernel(idx_ref, pstart_ref, dest_ref, carry):
    i = pl.program_id(0)

    @pl.when(i == 0)
    def _():
        carry[...] = jnp.zeros_like(carry)

    tm = idx_ref.shape[0]
    idx = idx_ref[...]
    lane = lax.broadcasted_iota(jnp.int32, idx.shape, 1)
    oh1 = lane == idx[:, 0:1]
    oh2 = lane == idx[:, 1:2]
    oh = jnp.where(oh1, 1.0, jnp.where(oh2, 1.0, 0.0))
    ii = lax.broadcasted_iota(jnp.int32, (tm, tm), 0)
    jj = lax.broadcasted_iota(jnp.int32, (tm, tm), 1)
    earlier = jnp.dot(jnp.where(ii > jj, 1.0, 0.0).astype(BF16), oh.astype(BF16), preferred_element_type=F32)
    base = earlier + (carry[...] + pstart_ref[...])
    d1 = jnp.sum(jnp.where(oh1, base, 0.0), axis=-1, keepdims=True)
    d2 = jnp.sum(jnp.where(oh2, base, 0.0), axis=-1, keepdims=True)
    dest_ref[...] = jnp.where(lane == 0, d1, jnp.where(lane == 1, d2, 0.0)).astype(jnp.int32)
    carry[...] += jnp.sum(oh, axis=0, keepdims=True)


def _slots(idx, pstart, tm):
    N = idx.shape[0]
    return pl.pallas_call(
        _slot_kernel, name="slots", grid=(N // tm,),
        in_specs=[pl.BlockSpec((tm, LANES), lambda i: (i, 0)), pl.BlockSpec((1, LANES), lambda i: (0, 0))],
        out_specs=pl.BlockSpec((tm, LANES), lambda i: (i, 0)),
        out_shape=jax.ShapeDtypeStruct((N, LANES), jnp.int32),
        scratch_shapes=[pltpu.VMEM((1, LANES), F32)],
        compiler_params=_cparams(("arbitrary",)),
    )(idx, pstart)


DMA_UNROLL = 8


def _dispatch_kernel(plo_ref, phi_ref, dest_hbm, h_ref, xs_ref, dsm, zrow, isem, sem):
    tm = h_ref.shape[0]
    i = pl.program_id(0)
    cp = pltpu.make_async_copy(dest_hbm.at[i], dsm, isem)
    cp.start()

    @pl.when(i == 0)
    def _():
        zrow[...] = jnp.zeros_like(zrow)
        for e in range(plo_ref.shape[0]):
            def zero_slot(d, c):
                pltpu.make_async_copy(zrow, xs_ref.at[d], sem).start()
                return c

            def zero_done(d, c):
                pltpu.make_async_copy(zrow, xs_ref.at[0], sem).wait()
                return c

            lax.fori_loop(plo_ref[e], phi_ref[e], zero_slot, 0)
            lax.fori_loop(plo_ref[e], phi_ref[e], zero_done, 0)

    cp.wait()

    def issue(r, c):
        pltpu.make_async_copy(h_ref.at[r], xs_ref.at[dsm[r]], sem).start(priority=0)
        pltpu.make_async_copy(h_ref.at[r], xs_ref.at[dsm[tm + r]], sem).start(priority=1)
        return c

    lax.fori_loop(0, tm, issue, 0, unroll=DMA_UNROLL)
    for _ in range(2):
        pltpu.make_async_copy(h_ref, xs_ref.at[pl.ds(0, tm)], sem).wait()


def _dispatch(pad_lo, pad_hi, dest_tiles, h2, rows, tm):
    N, ns, _ = h2.shape
    return pl.pallas_call(
        _dispatch_kernel, name="dispatch",
        grid_spec=pltpu.PrefetchScalarGridSpec(
            num_scalar_prefetch=2, grid=(N // tm,),
            in_specs=[pl.BlockSpec(memory_space=pl.ANY),
                      pl.BlockSpec((tm, ns, LANES), lambda i, lo, hi: (i, 0, 0))],
            out_specs=pl.BlockSpec(memory_space=pl.ANY),
            scratch_shapes=[pltpu.SMEM((2 * tm,), jnp.int32), pltpu.VMEM((ns, LANES), F32),
                            pltpu.SemaphoreType.DMA, pltpu.SemaphoreType.DMA]),
        out_shape=jax.ShapeDtypeStruct((rows, ns, LANES), F32),
        compiler_params=_cparams(("arbitrary",)),
    )(pad_lo, pad_hi, dest_tiles, h2)


def _expert_kernel(be_ref, nu_ref, xs_ref, wg_ref, wu_ref, wd_ref, ys_ref):
    del be_ref
    used = pl.program_id(0) < nu_ref[0]

    @pl.when(used)
    def _():
        xb = _from_token_tiles(xs_ref).astype(BF16)
        g = jnp.dot(xb, wg_ref[...], preferred_element_type=F32)
        u = jnp.dot(xb, wu_ref[...], preferred_element_type=F32)
        ys_ref[...] = _dot(_silu(g) * u, wd_ref[...])

    @pl.when(jnp.logical_not(used))
    def _():
        ys_ref[...] = jnp.zeros_like(ys_ref)


def _experts(block_e, n_used, xs, wg, wu, wd, j):
    rows, ns, _ = xs.shape
    D = ns * LANES
    F = wg.shape[3]
    nblk = rows // MOE_ROWS

    def row_map(i, be, nu):
        return (jnp.minimum(i, nu[0] - 1), 0, 0)

    def w_map(i, be, nu):
        return (j, be[i], 0, 0)

    return pl.pallas_call(
        _expert_kernel, name="experts",
        grid_spec=pltpu.PrefetchScalarGridSpec(
            num_scalar_prefetch=2, grid=(nblk,),
            in_specs=[pl.BlockSpec((MOE_ROWS, ns, LANES), row_map),
                      pl.BlockSpec((None, None, D, F), w_map), pl.BlockSpec((None, None, D, F), w_map),
                      pl.BlockSpec((None, None, F, D), w_map)],
            out_specs=pl.BlockSpec((MOE_ROWS, D), lambda i, be, nu: (i, 0))),
        out_shape=jax.ShapeDtypeStruct((rows, D), F32),
        compiler_params=_cparams(("arbitrary",)),
    )(block_e, n_used, xs, wg, wu, wd)


def _combine_kernel(dest_hbm, ys_hbm, gate_ref, x1_ref, g2_ref, o_ref, dsm, b0, b1, isem, sem):
    bb, tt, D = x1_ref.shape
    tm = bb * tt
    i = pl.program_id(0) * pl.num_programs(1) + pl.program_id(1)
    cp = pltpu.make_async_copy(dest_hbm.at[i], dsm, isem)
    cp.start()
    cp.wait()

    def issue(r, c):
        pltpu.make_async_copy(ys_hbm.at[pl.ds(dsm[r], 1)], b0.at[pl.ds(r, 1)], sem).start(priority=0)
        pltpu.make_async_copy(ys_hbm.at[pl.ds(dsm[tm + r], 1)], b1.at[pl.ds(r, 1)], sem).start(priority=1)
        return c

    lax.fori_loop(0, tm, issue, 0, unroll=DMA_UNROLL)
    pltpu.make_async_copy(ys_hbm.at[pl.ds(0, tm)], b0, sem).wait()
    pltpu.make_async_copy(ys_hbm.at[pl.ds(0, tm)], b1, sem).wait()
    g = gate_ref[...]
    y = g[:, 0:1] * b0[...] + g[:, 1:2] * b1[...]
    o_ref[...] = x1_ref[...] + g2_ref[...] * y.reshape(bb, tt, D)


def _combine(dest_tiles, ys, gates, x1, mod, l):
    B, T, D = x1.shape
    bb, tt = _tok_tiles(B, T)
    nt = T // tt
    tm = bb * tt
    ns = D // LANES
    return pl.pallas_call(
        _combine_kernel, name="combine", grid=(B // bb, nt),
        in_specs=[pl.BlockSpec(memory_space=pl.ANY), pl.BlockSpec(memory_space=pl.ANY),
                  pl.BlockSpec((tm, LANES), lambda b, t: (b * nt + t, 0)),
                  pl.BlockSpec((bb, tt, D), lambda b, t: (b, t, 0)),
                  _mod_spec(l, 5, bb, D)],
        out_specs=pl.BlockSpec((bb, tt, D), lambda b, t: (b, t, 0)),
        out_shape=jax.ShapeDtypeStruct((B, T, D), F32),
        scratch_shapes=[pltpu.SMEM((2 * tm,), jnp.int32), pltpu.VMEM((tm, D), F32),
                        pltpu.VMEM((tm, D), F32), pltpu.SemaphoreType.DMA, pltpu.SemaphoreType.DMA],
        compiler_params=_cparams(("arbitrary", "arbitrary")),
    )(dest_tiles, ys, gates, x1, mod)


def _moe(h2, x1, mod, wr, br, wg, wu, wd, l, jl):
    B, T, D = x1.shape
    bb, tt = _tok_tiles(B, T)
    tm = bb * tt
    N = B * T
    idx, gates, cnt = _router(h2, wr, br, tm, jl)
    counts = cnt[0, :N_EXPERTS].astype(jnp.int32)
    padded = (counts + MOE_ROWS - 1) // MOE_ROWS * MOE_ROWS
    pends = jnp.cumsum(padded)
    pstart = pends - padded
    nblk = -(-(2 * N) // MOE_ROWS) + N_EXPERTS
    starts = jnp.arange(nblk, dtype=jnp.int32) * MOE_ROWS
    block_e = jnp.minimum(jnp.sum((pends[None, :] <= starts[:, None]).astype(jnp.int32), axis=1), N_EXPERTS - 1)
    n_used = (pends[-1:] // MOE_ROWS).astype(jnp.int32)
    pstart_vec = jnp.zeros((1, LANES), F32).at[0, :N_EXPERTS].set(pstart.astype(F32))
    dest = _slots(idx, pstart_vec, tm)
    dest_tiles = dest[:, :2].reshape(N // tm, tm, 2).transpose(0, 2, 1).reshape(N // tm, 2 * tm)
    rows = nblk * MOE_ROWS
    pad_lo = jnp.concatenate([pstart + counts, pends[-1:]])
    pad_hi = jnp.concatenate([pends, jnp.full((1,), rows, jnp.int32)])
    xs = _dispatch(pad_lo, pad_hi, dest_tiles, h2, rows, tm)
    ys = _experts(block_e, n_used, xs, wg, wu, wd, jl)
    return _combine(dest_tiles, ys, gates, x1, mod, l)


def _final_kernel(x_ref, g_ref, o_ref):
    x = x_ref[...]
    o_ref[...] = x * lax.rsqrt(jnp.mean(x * x, axis=-1, keepdims=True) + EPS) * g_ref[...]


def _final(x, g):
    B, T, D = x.shape
    bb, tt = _tok_tiles(B, T)
    return pl.pallas_call(
        _final_kernel, name="final", grid=(B // bb, T // tt),
        in_specs=[pl.BlockSpec((bb, tt, D), lambda b, t: (b, t, 0)), pl.BlockSpec((1, D), lambda b, t: (0, 0))],
        out_specs=pl.BlockSpec((bb, tt, D), lambda b, t: (b, t, 0)),
        out_shape=jax.ShapeDtypeStruct((B, T, D), F32),
        compiler_params=_cparams(("parallel", "parallel")),
    )(x, g)


def _trunk(x, mods, st, p):
    L = mods.shape[0]
    new = [[] for _ in range(5)]
    h0_all = st[4][:, :, None, :]
    for l in range(L):
        dense = l % 2 == 0
        j = l // 2
        u, qkv, z, lru, ba = _inproj(x, mods, p['norm1'], p['w_in'], p['w_in_tail'], l)
        ya, n_a = _mixa(u, st[0], p['conv_a_w'], p['conv_a_b'], p['ln_a_g'], p['ln_a_b'], l)
        yb, n_q, n_d = _gdn(qkv, z, ba, st[1], st[2], p['conv_qkv_w'], p['alog'], p['dtb'], p['gdn_norm'], l)
        yc, n_cl, n_h = _lru(lru, st[3], h0_all, p['conv_lru_w'], p['conv_lru_b'], p['lru_wa'], p['lru_b_a'],
                             p['lru_wi'], p['lru_b_i'], p['lru_lambda'], p['lru_norm'], l)
        x1, h2 = _outproj(ya, yb, yc, x, mods, p['norm2'], p['w_out'], l, tile_out=not dense)
        if dense:
            x = _ffn(h2, x1, mods, p['ffn_wg'], p['ffn_wu'], p['ffn_wd'], l, j)
        else:
            x = _moe(h2, x1, mods, p['router_w'], p['router_b'], p['moe_wg'], p['moe_wu'], p['moe_wd'], l, j)
        for lst, val in zip(new, (n_a, n_q, n_d, n_cl, n_h[:, 0, :])):
            lst.append(val)
    y = _final(x, p['final_norm'])
    return y, [jnp.stack(v, axis=0) for v in new]


def kernel(x_prompt, x_sample, c_prompt, c_sample, state_conv_a, state_conv_qkv, state_delta, state_conv_lru, state_lru, w_ada, b_ada, norm1, norm2, w_in, w_out, conv_a_w, conv_a_b, ln_a_g, ln_a_b, conv_qkv_w, gdn_a_log, gdn_dt_bias, gdn_norm, conv_lru_w, conv_lru_b, lru_w_a, lru_b_a, lru_w_i, lru_b_i, lru_lambda, lru_norm, ffn_wg, ffn_wu, ffn_wd, router_w, router_b, moe_wg, moe_wu, moe_wd, final_norm):
    L, D = norm1.shape
    Bp, Bs = x_prompt.shape[0], x_sample.shape[0]
    nh = gdn_a_log.shape[1]

    def row(v):
        return v[:, None, :]

    def lanes(v):
        return jnp.zeros((L, 1, LANES), F32).at[:, 0, nh:2 * nh].set(v)

    def block_diag(w):
        n, k = w.shape[1], w.shape[2]
        eye = jnp.eye(n, dtype=w.dtype)
        return jnp.einsum('lnij,nm->lnimj', w, eye).reshape(L, n * k, n * k)

    w_in_tail = jnp.concatenate([w_in[:, :, _IN_HEAD + 2 * nh:], w_in[:, :, _IN_HEAD:_IN_HEAD + 2 * nh],
                                 jnp.zeros((L, D, LANES - 2 * nh), w_in.dtype)], axis=-1).astype(BF16)
    n_moe = router_w.shape[0]
    p = dict(
        norm1=row(norm1), norm2=row(norm2), w_in=w_in[:, :, :_IN_HEAD].astype(BF16), w_in_tail=w_in_tail,
        w_out=w_out.astype(BF16),
        conv_a_w=conv_a_w, conv_a_b=row(conv_a_b), ln_a_g=row(ln_a_g), ln_a_b=row(ln_a_b),
        conv_qkv_w=conv_qkv_w, alog=lanes(gdn_a_log), dtb=lanes(gdn_dt_bias), gdn_norm=row(gdn_norm),
        conv_lru_w=conv_lru_w, conv_lru_b=row(conv_lru_b), lru_wa=block_diag(lru_w_a), lru_b_a=row(lru_b_a),
        lru_wi=block_diag(lru_w_i), lru_b_i=row(lru_b_i), lru_lambda=row(lru_lambda), lru_norm=row(lru_norm),
        ffn_wg=ffn_wg.astype(BF16), ffn_wu=ffn_wu.astype(BF16), ffn_wd=ffn_wd.astype(BF16),
        router_w=jnp.concatenate([router_w, jnp.zeros((n_moe, D, LANES - N_EXPERTS), F32)], axis=-1),
        router_b=jnp.concatenate([router_b, jnp.full((n_moe, LANES - N_EXPERTS), NEG_BIG, F32)],
                                 axis=-1)[:, None, :],
        moe_wg=moe_wg.astype(BF16), moe_wu=moe_wu.astype(BF16), moe_wd=moe_wd.astype(BF16),
        final_norm=final_norm[None, :],
    )

    c_all = jnp.concatenate([c_prompt, c_sample], axis=0)
    mod = _ada(c_all, w_ada, b_ada)
    mod = mod.reshape(L, Bp + Bs, 6, D).transpose(0, 2, 1, 3)[:, :, :, None, :]
    mod_p, mod_s = mod[:, :, :Bp], mod[:, :, Bp:]

    zero_states = [jnp.zeros((L, Bp) + s.shape[2:], F32)
                   for s in (state_conv_a, state_conv_qkv, state_delta, state_conv_lru, state_lru)]
    y_p, st_p = _trunk(x_prompt, mod_p, zero_states, p)
    y_s, st_s = _trunk(x_sample, mod_s,
                       [state_conv_a, state_conv_qkv, state_delta, state_conv_lru, state_lru], p)
    return (y_p, y_s, *st_p, *st_s)
```

```python
import functools

import jax
import jax.numpy as jnp
from jax import lax
from jax.experimental import pallas as pl
from jax.experimental.pallas import tpu as pltpu

F32 = jnp.float32
BF16 = jnp.bfloat16
EPS = 1e-6

V7X_VMEM_LIMIT_BYTES = 56 * 1024 * 1024
LANES = 128
TOKEN_TILE = 512
MOE_ROWS = 512
N_EXPERTS = 8
GDN_CHUNK = 64
LRU_C = 8.0
NEG_BIG = -1e30


def _cparams(sem):
    return pltpu.CompilerParams(dimension_semantics=sem, vmem_limit_bytes=V7X_VMEM_LIMIT_BYTES)


def _sigmoid(x):
    return jax.nn.sigmoid(x)


def _silu(x):
    return x * jax.nn.sigmoid(x)


def _softplus(x):
    return jnp.maximum(x, 0.0) + jnp.log1p(jnp.exp(-jnp.abs(x)))


def _dot(a, b):
    return jnp.dot(a.astype(BF16), b.astype(BF16), preferred_element_type=F32)


def _dot_nt(a, b):
    return lax.dot_general(a.astype(BF16), b.astype(BF16), (((1,), (1,)), ((), ())),
                           preferred_element_type=F32)


def _split2(x):
    hi = x.astype(BF16)
    lo = (x - hi.astype(F32)).astype(BF16)
    return hi, lo


def _split3(x):
    hi = x.astype(BF16)
    r = x - hi.astype(F32)
    mid = r.astype(BF16)
    lo = (r - mid.astype(F32)).astype(BF16)
    return hi, mid, lo


def _dot_hp(a, b):
    ah, al = _split2(a)
    bh, bl = _split2(b)
    d = functools.partial(jnp.dot, preferred_element_type=F32)
    return d(ah, bh) + (d(ah, bl) + d(al, bh))


def _dot_exact_lhs(a_bf16, b):
    d = functools.partial(jnp.dot, preferred_element_type=F32)
    h, m, l = _split3(b)
    return d(a_bf16, h) + (d(a_bf16, m) + d(a_bf16, l))


def _tok_tiles(B, T, target=TOKEN_TILE):
    if T >= target:
        assert T % target == 0
        return 1, target
    bb = max(1, min(B, target // T))
    assert B % bb == 0
    return bb, T


def _ada_kernel(c_ref, w_ref, b_ref, o_ref):
    c = c_ref[...]
    o_ref[...] = _dot(_silu(c), w_ref[...]) + b_ref[...]


def _ada(c_all, w_ada, b_ada):
    L, D, D6 = w_ada.shape
    Bc = c_all.shape[0]
    tn = 1536
    return pl.pallas_call(
        _ada_kernel, name="ada", grid=(L, D6 // tn),
        in_specs=[pl.BlockSpec((Bc, D), lambda l, j: (0, 0)),
                  pl.BlockSpec((None, D, tn), lambda l, j: (l, 0, j)),
                  pl.BlockSpec((None, 1, tn), lambda l, j: (l, 0, j))],
        out_specs=pl.BlockSpec((None, Bc, tn), lambda l, j: (l, 0, j)),
        out_shape=jax.ShapeDtypeStruct((L, Bc, D6), F32),
        compiler_params=_cparams(("parallel", "parallel")),
    )(c_all, w_ada, b_ada.reshape(L, 1, D6))


def _mod_spec(l, j, bb, D):
    return pl.BlockSpec((None, None, bb, 1, D), lambda b, *_: (l, j, b, 0, 0))


def _layer_spec(l, shape):
    zeros = (0,) * len(shape)
    return pl.BlockSpec((None,) + tuple(shape), lambda *_: (l,) + zeros)


def _state_spec(l, bb, shape):
    zeros = (0,) * len(shape)
    return pl.BlockSpec((None, bb) + tuple(shape), lambda b, *_: (l, b) + zeros)


_IN_AG = (0, 512)
_IN_QKV = (512, 2048)
_IN_Z = (2048, 2560)
_IN_HEAD = 2560
_TAIL_LRU = (0, 512)
_TAIL_BA = (512, 640)
_IN_TAIL = 640


def _inproj_kernel(x_ref, sc_ref, sh_ref, g_ref, w_ref, wt_ref, u_ref, qkv_ref, z_ref, lru_ref, ba_ref):
    bb, tt, D = x_ref.shape
    x = x_ref[...]
    h = x * lax.rsqrt(jnp.mean(x * x, axis=-1, keepdims=True) + EPS) * g_ref[...]
    h = h * (1.0 + sc_ref[...]) + sh_ref[...]
    hb = h.reshape(bb * tt, D).astype(BF16)

    def seg(lo_hi, ref=w_ref):
        return jnp.dot(hb, ref[:, lo_hi[0]:lo_hi[1]], preferred_element_type=F32)

    ag = seg(_IN_AG)
    u_ref[...] = (ag[:, :256] * _sigmoid(ag[:, 256:])).reshape(bb, tt, 256)
    qkv_ref[...] = seg(_IN_QKV).reshape(bb, tt, 1536)
    z_ref[...] = seg(_IN_Z).reshape(bb, tt, 512)
    lru_ref[...] = seg(_TAIL_LRU, wt_ref).reshape(bb, tt, 512)
    ba_ref[...] = seg(_TAIL_BA, wt_ref).reshape(bb, tt, LANES)


def _inproj(x, mod, gain, w, wt, l):
    B, T, D = x.shape
    bb, tt = _tok_tiles(B, T)
    widths = (256, 1536, 512, 512, LANES)
    return pl.pallas_call(
        _inproj_kernel, name="inproj", grid=(B // bb, T // tt),
        in_specs=[pl.BlockSpec((bb, tt, D), lambda b, t: (b, t, 0)),
                  _mod_spec(l, 1, bb, D), _mod_spec(l, 0, bb, D),
                  _layer_spec(l, (1, D)), _layer_spec(l, (D, _IN_HEAD)), _layer_spec(l, (D, _IN_TAIL))],
        out_specs=[pl.BlockSpec((bb, tt, n), lambda b, t: (b, t, 0)) for n in widths],
        out_shape=[jax.ShapeDtypeStruct((B, T, n), F32) for n in widths],
        compiler_params=_cparams(("parallel", "parallel")),
    )(x, mod, mod, gain, w, wt)


def _mixa_kernel(u_ref, buf_ref, w_ref, cb_ref, lg_ref, lb_ref, y_ref, ns_ref, xp_ref):
    bb, tc, ch = u_ref.shape
    width = w_ref.shape[0]
    t = pl.program_id(1)

    @pl.when(t == 0)
    def _():
        xp_ref[:, 2:32, :] = buf_ref[...]

    @pl.when(t > 0)
    def _():
        xp_ref[:, 0:32, :] = xp_ref[:, tc:tc + 32, :]

    xp_ref[:, 32:32 + tc, :] = u_ref[...]
    ns_ref[...] = xp_ref[:, tc + 2:tc + 32, :]

    rs = min(tc, 64)
    win = rs + 32
    for b in range(bb):
        for r0 in range(0, tc, rs):
            w0 = xp_ref[b, r0:r0 + win, :]
            shifted = [w0] + [pltpu.roll(w0, win - r, 0) for r in range(1, 8)]
            acc = jnp.zeros((rs, ch), F32) + cb_ref[...]
            for j in range(width):
                a, r = divmod(2 + j, 8)
                acc = acc + w_ref[j:j + 1, :] * shifted[r][8 * a:8 * a + rs, :]
            mu = jnp.mean(acc, axis=-1, keepdims=True)
            d = acc - mu
            var = jnp.mean(d * d, axis=-1, keepdims=True)
            yn = d * lax.rsqrt(var + EPS) * lg_ref[...] + lb_ref[...]
            y_ref[b, r0:r0 + rs, :] = _silu(yn)


def _mixa(u, buf, w, cb, lg, lb, l):
    B, T, ch = u.shape
    if T >= 512:
        bb, tc = 1, 512
    elif T >= 256:
        bb, tc = 1, 256
    else:
        bb, tc = min(B, 8), T
    assert T % tc == 0 and B % bb == 0
    nb = buf.shape[2]
    vec = _layer_spec(l, (1, ch))
    return pl.pallas_call(
        _mixa_kernel, name="mixa", grid=(B // bb, T // tc),
        in_specs=[pl.BlockSpec((bb, tc, ch), lambda b, t: (b, t, 0)),
                  _state_spec(l, bb, (nb, ch)),
                  _layer_spec(l, w.shape[1:]), vec, vec, vec],
        out_specs=[pl.BlockSpec((bb, tc, ch), lambda b, t: (b, t, 0)),
                   pl.BlockSpec((bb, nb, ch), lambda b, t: (b, 0, 0))],
        out_shape=[jax.ShapeDtypeStruct((B, T, ch), F32), jax.ShapeDtypeStruct((B, nb, ch), F32)],
        scratch_shapes=[pltpu.VMEM((bb, 32 + tc, ch), F32)],
        compiler_params=_cparams(("parallel", "arbitrary")),
    )(u, buf, w, cb, lg, lb)


def _gdn_kernel(qkv_ref, z_ref, ba_ref, sq_ref, sd_ref, cw_ref, alog_ref, dtb_ref, gn_ref, nds_ref,
                y_ref, nq_ref, nd_ref, xp_ref, qn_ref, bg_ref, s_ref, *, C):
    del nds_ref
    bb, tt, width = qkv_ref.shape
    nh = sd_ref.shape[1]
    dk = sd_ref.shape[2]
    ntap = cw_ref.shape[0]
    t = pl.program_id(1)

    @pl.when(t == 0)
    def _():
        xp_ref[:, 5:8, :] = sq_ref[...]
        s_ref[...] = sd_ref[...]

    @pl.when(t > 0)
    def _():
        xp_ref[:, 5:8, :] = xp_ref[:, tt + 5:tt + 8, :]

    xp_ref[:, 8:8 + tt, :] = qkv_ref[...]
    nq_ref[...] = xp_ref[:, tt + 5:tt + 8, :]

    rs = min(tt, 128)
    win = rs + 8
    for b in range(bb):
        for sl in range(width // dk):
            lo = sl * dk
            for r0 in range(0, tt, rs):
                w0 = xp_ref[b, r0:r0 + win, lo:lo + dk]
                acc = cw_ref[ntap - 1:ntap, lo:lo + dk] * w0[8:8 + rs, :]
                for j in range(ntap - 1):
                    acc = acc + cw_ref[j:j + 1, lo:lo + dk] * pltpu.roll(w0, win - 5 - j, 0)[:rs, :]
                s = _silu(acc)
                if sl < 2 * nh:
                    s = s * lax.rsqrt(jnp.sum(s * s, axis=-1, keepdims=True) + EPS)
                if sl < nh:
                    s = s * (dk ** -0.5)
                qn_ref[b, r0:r0 + rs, lo:lo + dk] = s

    ba = ba_ref[...]
    lane3 = lax.broadcasted_iota(jnp.int32, ba.shape, 2)
    gdec = -jnp.exp(alog_ref[...]) * _softplus(ba + dtb_ref[...])
    bg_ref[...] = jnp.where(lane3 < nh, _sigmoid(ba), jnp.where(lane3 < 2 * nh, gdec, 0.0))

    ii = lax.broadcasted_iota(jnp.int32, (C, C), 0)
    jj = lax.broadcasted_iota(jnp.int32, (C, C), 1)
    tri_incl = jnp.where(ii >= jj, 1.0, 0.0).astype(BF16)
    lane_c = lax.broadcasted_iota(jnp.int32, (C, LANES), 1)
    nsteps = C.bit_length() - 1
    assert (1 << nsteps) == C
    ident = jnp.where(ii == jj, 1.0, 0.0)
    pair_mask = []
    for s in range(nsteps):
        same_big = jnp.right_shift(ii, s + 1) == jnp.right_shift(jj, s + 1)
        diff_small = jnp.right_shift(ii, s) != jnp.right_shift(jj, s)
        pair_mask.append((same_big, diff_small))

    def pair_part(s, m):
        return jnp.where(pair_mask[s][0], jnp.where(pair_mask[s][1], m, 0.0), 0.0)

    nc = tt // C
    probs = [(b, c, h) for b in range(bb) for c in range(nc) for h in range(nh)]
    bgc, gcs, gct = {}, {}, {}
    for b in range(bb):
        for c in range(nc):
            bgc[b, c] = bg_ref[b, c * C:(c + 1) * C, :]
            gcs[b, c] = _dot_exact_lhs(tri_incl, jnp.where(lane_c < nh, 0.0, bgc[b, c]))
    for key in gcs:
        gct[key] = gcs[key].T

    def rows(b, c, slab):
        return qn_ref[b, c * C:(c + 1) * C, slab * dk:(slab + 1) * dk]

    q = {p: rows(p[0], p[1], p[2]) for p in probs}
    k = {p: rows(p[0], p[1], nh + p[2]) for p in probs}
    gcol = {(b, c, h): gcs[b, c][:, nh + h:nh + h + 1] for b, c, h in probs}
    kb = {(b, c, h): k[b, c, h] * bgc[b, c][:, h:h + 1] for b, c, h in probs}
    a2 = {p: _dot_nt(jnp.concatenate([kb[p], q[p]], axis=0), k[p]) for p in probs}
    decay = {(b, c, h): jnp.exp(jnp.where(ii >= jj, gcol[b, c, h] - gct[b, c][nh + h:nh + h + 1, :], -jnp.inf))
             for b, c, h in probs}
    low = {p: jnp.where(ii > jj, a2[p][:C] * decay[p], 0.0) for p in probs}
    tinv = {p: ident - pair_part(0, low[p]) for p in probs}
    for lvl in range(1, nsteps):
        half = {p: _dot(tinv[p], pair_part(lvl, low[p])) for p in probs}
        tinv = {p: tinv[p] - _dot(half[p], tinv[p]) for p in probs}
    rhs = {(b, c, h): jnp.concatenate([rows(b, c, 2 * nh + h) * bgc[b, c][:, h:h + 1],
                                        kb[b, c, h] * jnp.exp(gcol[b, c, h])], axis=1) for b, c, h in probs}
    uw = {p: _dot(tinv[p], rhs[p]) for p in probs}

    state = {(b, h): s_ref[b, h] for b in range(bb) for h in range(nh)}
    for c in range(nc):
        cur = [(b, c, h) for b in range(bb) for h in range(nh)]
        wq = {p: _dot(jnp.concatenate([uw[p][:, dk:], q[p] * jnp.exp(gcol[p])], axis=0), state[p[0], p[2]])
              for p in cur}
        v_new = {p: uw[p][:, :dk] - wq[p][:C] for p in cur}
        o = {p: wq[p][C:] + _dot(a2[p][C:] * decay[p], v_new[p]) for p in cur}
        for b, _, h in cur:
            p = (b, c, h)
            gl = gcs[b, c][C - 1:C, nh + h:nh + h + 1]
            kdec = k[p] * jnp.exp(gl - gcol[p])
            upd = lax.dot_general(kdec.astype(BF16), v_new[p].astype(BF16), (((0,), (0,)), ((), ())),
                                  preferred_element_type=F32)
            state[b, h] = state[b, h] * jnp.exp(gl) + upd
        for b, _, h in cur:
            p = (b, c, h)
            on = o[p] * lax.rsqrt(jnp.mean(o[p] * o[p], axis=-1, keepdims=True) + EPS) * gn_ref[...]
            zz = z_ref[b, c * C:(c + 1) * C, h * dk:(h + 1) * dk]
            y_ref[b, c * C:(c + 1) * C, h * dk:(h + 1) * dk] = on * _silu(zz)
    for (b, h), val in state.items():
        s_ref[b, h] = val
    nd_ref[...] = s_ref[...]


def _gdn(qkv, z, ba, s_qkv, s_delta, cw, alog_vec, dtb_vec, gn, nd_stack, l):
    B, T, width = qkv.shape
    nh, dk = s_delta.shape[2], s_delta.shape[3]
    C = min(GDN_CHUNK, T)
    if T > C:
        bb = min(B, 2)
        tt = min(T, 8 * C // bb)
    else:
        bb, tt = min(B, 8), T
    assert T % tt == 0 and tt % C == 0 and B % bb == 0
    vec = _layer_spec(l, (1, LANES))
    kern = functools.partial(_gdn_kernel, C=C)
    return pl.pallas_call(
        kern, name="gdn", grid=(B // bb, T // tt),
        in_specs=[pl.BlockSpec((bb, tt, width), lambda b, t: (b, t, 0)),
                  pl.BlockSpec((bb, tt, nh * dk), lambda b, t: (b, t, 0)),
                  pl.BlockSpec((bb, tt, LANES), lambda b, t: (b, t, 0)),
                  _state_spec(l, bb, (3, width)),
                  _state_spec(l, bb, (nh, dk, dk)),
                  _layer_spec(l, cw.shape[1:]), vec, vec, vec, pl.BlockSpec(memory_space=pl.ANY)],
        out_specs=[pl.BlockSpec((bb, tt, nh * dk), lambda b, t: (b, t, 0)),
                   pl.BlockSpec((bb, 3, width), lambda b, t: (b, 0, 0)),
                   _state_spec(l, bb, (nh, dk, dk))],
        out_shape=[jax.ShapeDtypeStruct((B, T, nh * dk), F32),
                   jax.ShapeDtypeStruct((B, 3, width), F32),
                   jax.ShapeDtypeStruct(nd_stack.shape, F32)],
        input_output_aliases={9: 2},
        scratch_shapes=[pltpu.VMEM((bb, 8 + tt, width), F32), pltpu.VMEM((bb, tt, width), F32),
                        pltpu.VMEM((bb, tt, LANES), F32), pltpu.VMEM((bb, nh, dk, dk), F32)],
        compiler_params=_cparams(("parallel", "arbitrary")),
    )(qkv, z, ba, s_qkv, s_delta, cw, alog_vec, dtb_vec, gn, nd_stack)


def _lru_kernel(lru_ref, cs_ref, h0_ref, cw_ref, cb_ref, wa_ref, ba_ref, wi_ref, bi_ref, lam_ref, ng_ref,
                y_ref, ncs_ref, nh_ref, xp_ref, sa_ref, sb_ref, h_ref):
    bb, tc, two_ch = lru_ref.shape
    ch = two_ch // 2
    ntap = cw_ref.shape[0]
    t = pl.program_id(1)

    @pl.when(t == 0)
    def _():
        xp_ref[:, 5:8, :] = cs_ref[...]
        h_ref[...] = h0_ref[...]

    @pl.when(t > 0)
    def _():
        xp_ref[:, 5:8, :] = xp_ref[:, tc + 5:tc + 8, :]

    xp_ref[:, 8:8 + tc, :] = lru_ref[:, :, 0:ch]
    ncs_ref[...] = xp_ref[:, tc + 5:tc + 8, :]

    xc = jnp.zeros((bb, tc, ch), F32) + cb_ref[...]
    for j in range(ntap):
        xc = xc + cw_ref[j:j + 1, :] * xp_ref[:, 5 + j:5 + j + tc, :]
    x2 = xc.reshape(bb * tc, ch)
    r = _sigmoid(_dot_hp(x2, wa_ref[...]) + ba_ref[...])
    i = _sigmoid(_dot_hp(x2, wi_ref[...]) + bi_ref[...])
    log_a = (-LRU_C * r) * _softplus(-lam_ref[...])
    a = jnp.exp(log_a)
    bv = jnp.sqrt(1.0 - jnp.exp(2.0 * log_a)) * (i * x2)
    a3 = a.reshape(bb, tc, ch)
    b3 = bv.reshape(bb, tc, ch)

    sa_ref[:, 0:tc, :] = jnp.ones((bb, tc, ch), F32)
    sb_ref[:, 0:tc, :] = jnp.zeros((bb, tc, ch), F32)
    d = 1
    while d < tc:
        sa_ref[:, tc:2 * tc, :] = a3
        sb_ref[:, tc:2 * tc, :] = b3
        ap = sa_ref[:, tc - d:2 * tc - d, :]
        bp = sb_ref[:, tc - d:2 * tc - d, :]
        b3 = a3 * bp + b3
        a3 = a3 * ap
        d *= 2
    h = a3 * h_ref[...] + b3
    h_last = h[:, tc - 1:tc, :]
    h_ref[...] = h_last
    nh_ref[...] = h_last

    y = h * jax.nn.gelu(lru_ref[:, :, ch:two_ch])
    y_ref[...] = y * lax.rsqrt(jnp.mean(y * y, axis=-1, keepdims=True) + EPS) * ng_ref[...]


def _lru(lru, cs, h0, cw, cb, wa, ba, wi, bi, lam, ng, l):
    B, T, two_ch = lru.shape
    ch = two_ch // 2
    if T >= 512:
        bb, tc = 1, 512
    elif T >= 256:
        bb, tc = 1, 256
    else:
        bb, tc = min(B, 16), T
    assert T % tc == 0 and B % bb == 0
    vec = _layer_spec(l, (1, ch))
    mat = _layer_spec(l, (ch, ch))
    return pl.pallas_call(
        _lru_kernel, name="lru", grid=(B // bb, T // tc),
        in_specs=[pl.BlockSpec((bb, tc, two_ch), lambda b, t: (b, t, 0)),
                  _state_spec(l, bb, (3, ch)),
                  _state_spec(l, bb, (1, ch)),
                  _layer_spec(l, cw.shape[1:]), vec, mat, vec, mat, vec, vec, vec],
        out_specs=[pl.BlockSpec((bb, tc, ch), lambda b, t: (b, t, 0)),
                   pl.BlockSpec((bb, 3, ch), lambda b, t: (b, 0, 0)),
                   pl.BlockSpec((bb, 1, ch), lambda b, t: (b, 0, 0))],
        out_shape=[jax.ShapeDtypeStruct((B, T, ch), F32), jax.ShapeDtypeStruct((B, 3, ch), F32),
                   jax.ShapeDtypeStruct((B, 1, ch), F32)],
        scratch_shapes=[pltpu.VMEM((bb, 8 + tc, ch), F32), pltpu.VMEM((bb, 2 * tc, ch), F32),
                        pltpu.VMEM((bb, 2 * tc, ch), F32), pltpu.VMEM((bb, 1, ch), F32)],
        compiler_params=_cparams(("parallel", "arbitrary")),
    )(lru, cs, h0, cw, cb, wa, ba, wi, bi, lam, ng)


def _outproj_kernel(ya_ref, yb_ref, yc_ref, x_ref, g1_ref, sc_ref, sh_ref, n2_ref, w_ref, x1_ref, h2_ref):
    bb, tt, D = x_ref.shape
    m = bb * tt
    na, nb = ya_ref.shape[2], yb_ref.shape[2]
    mix = (_dot(ya_ref[...].reshape(m, na), w_ref[0:na, :])
           + _dot(yb_ref[...].reshape(m, nb), w_ref[na:na + nb, :])
           + _dot(yc_ref[...].reshape(m, D - na - nb), w_ref[na + nb:D, :]))
    x1 = x_ref[...] + g1_ref[...] * mix.reshape(bb, tt, D)
    x1_ref[...] = x1
    h = x1 * lax.rsqrt(jnp.mean(x1 * x1, axis=-1, keepdims=True) + EPS) * n2_ref[...]
    h = (h * (1.0 + sc_ref[...]) + sh_ref[...]).reshape(m, D)
    if len(h2_ref.shape) == 2:
        h2_ref[...] = h
    else:
        _to_token_tiles(h, h2_ref)


def _to_token_tiles(val, ref):
    for s in range(ref.shape[1]):
        ref[:, s, :] = val[:, s * LANES:(s + 1) * LANES]


def _from_token_tiles(ref):
    return jnp.concatenate([ref[:, s, :] for s in range(ref.shape[1])], axis=1)


def _outproj(ya, yb, yc, x, mod, gain2, w, l, tile_out):
    B, T, D = x.shape
    bb, tt = _tok_tiles(B, T)
    nt = T // tt

    def tok(n):
        return pl.BlockSpec((bb, tt, n), lambda b, t: (b, t, 0))

    if tile_out:
        h2_spec = pl.BlockSpec((bb * tt, D // LANES, LANES), lambda b, t: (b * nt + t, 0, 0))
        h2_shape = jax.ShapeDtypeStruct((B * T, D // LANES, LANES), F32)
    else:
        h2_spec = pl.BlockSpec((bb * tt, D), lambda b, t: (b * nt + t, 0))
        h2_shape = jax.ShapeDtypeStruct((B * T, D), F32)
    return pl.pallas_call(
        _outproj_kernel, name="outproj", grid=(B // bb, nt),
        in_specs=[tok(ya.shape[2]), tok(yb.shape[2]), tok(yc.shape[2]), tok(D),
                  _mod_spec(l, 2, bb, D), _mod_spec(l, 4, bb, D), _mod_spec(l, 3, bb, D),
                  _layer_spec(l, (1, D)), _layer_spec(l, (D, D))],
        out_specs=[tok(D), h2_spec],
        out_shape=[jax.ShapeDtypeStruct((B, T, D), F32), h2_shape],
        compiler_params=_cparams(("parallel", "parallel")),
    )(ya, yb, yc, x, mod, mod, mod, gain2, w)


def _ffn_kernel(h_ref, x1_ref, g2_ref, wg_ref, wu_ref, wd_ref, o_ref, acc_ref):
    bb, tt, D = x1_ref.shape
    j = pl.program_id(2)

    @pl.when(j == 0)
    def _():
        acc_ref[...] = jnp.zeros_like(acc_ref)

    hb = h_ref[...].astype(BF16)
    g = jnp.dot(hb, wg_ref[...], preferred_element_type=F32)
    u = jnp.dot(hb, wu_ref[...], preferred_element_type=F32)
    acc_ref[...] += _dot(_silu(g) * u, wd_ref[...])

    @pl.when(j == pl.num_programs(2) - 1)
    def _():
        o_ref[...] = x1_ref[...] + g2_ref[...] * acc_ref[...].reshape(bb, tt, D)


def _ffn(h2, x1, mod, wg, wu, wd, l, jl):
    B, T, D = x1.shape
    bb, tt = _tok_tiles(B, T)
    nt = T // tt
    F = wg.shape[2]
    tf = F // 2
    assert tf % LANES == 0
    return pl.pallas_call(
        _ffn_kernel, name="ffn", grid=(B // bb, nt, F // tf),
        in_specs=[pl.BlockSpec((bb * tt, D), lambda b, t, j: (b * nt + t, 0)),
                  pl.BlockSpec((bb, tt, D), lambda b, t, j: (b, t, 0)),
                  _mod_spec(l, 5, bb, D),
                  pl.BlockSpec((None, D, tf), lambda b, t, j: (jl, 0, j)),
                  pl.BlockSpec((None, D, tf), lambda b, t, j: (jl, 0, j)),
                  pl.BlockSpec((None, tf, D), lambda b, t, j: (jl, j, 0))],
        out_specs=pl.BlockSpec((bb, tt, D), lambda b, t, j: (b, t, 0)),
        out_shape=jax.ShapeDtypeStruct((B, T, D), F32),
        scratch_shapes=[pltpu.VMEM((bb * tt, D), F32)],
        compiler_params=_cparams(("parallel", "parallel", "arbitrary")),
    )(h2, x1, mod, wg, wu, wd)


def _router_kernel(h_ref, wr_ref, br_ref, idx_ref, gate_ref, cnt_ref, cnt_acc):
    i = pl.program_id(0)

    @pl.when(i == 0)
    def _():
        cnt_acc[...] = jnp.zeros_like(cnt_acc)

    logits = _dot_hp(_from_token_tiles(h_ref), wr_ref[...]) + br_ref[...]
    lane = lax.broadcasted_iota(jnp.int32, logits.shape, 1).astype(F32)
    m1 = jnp.max(logits, axis=-1, keepdims=True)
    i1 = jnp.min(jnp.where(logits == m1, lane, float(LANES)), axis=-1, keepdims=True)
    rest = jnp.where(lane == i1, NEG_BIG, logits)
    m2 = jnp.max(rest, axis=-1, keepdims=True)
    i2 = jnp.min(jnp.where(rest == m2, lane, float(LANES)), axis=-1, keepdims=True)
    e = jnp.exp(m2 - m1)
    g1 = 1.0 / (1.0 + e)
    g2 = e / (1.0 + e)
    idx_ref[...] = jnp.where(lane == 0.0, i1, jnp.where(lane == 1.0, i2, 0.0)).astype(jnp.int32)
    gate_ref[...] = jnp.where(lane == 0.0, g1, jnp.where(lane == 1.0, g2, 0.0))
    oh = jnp.where(lane == i1, 1.0, jnp.where(lane == i2, 1.0, 0.0))
    cnt_acc[...] += jnp.sum(oh, axis=0, keepdims=True)
    cnt_ref[...] = jnp.broadcast_to(cnt_acc[...], cnt_ref.shape)


def _router(h2, wr, br, tm, jl):
    N, ns, _ = h2.shape
    D = ns * LANES
    return pl.pallas_call(
        _router_kernel, name="router", grid=(N // tm,),
        in_specs=[pl.BlockSpec((tm, ns, LANES), lambda i: (i, 0, 0)),
                  _layer_spec(jl, (D, LANES)), _layer_spec(jl, (1, LANES))],
        out_specs=[pl.BlockSpec((tm, LANES), lambda i: (i, 0)),
                   pl.BlockSpec((tm, LANES), lambda i: (i, 0)),
                   pl.BlockSpec((8, LANES), lambda i: (0, 0))],
        out_shape=[jax.ShapeDtypeStruct((N, LANES), jnp.int32), jax.ShapeDtypeStruct((N, LANES), F32),
                   jax.ShapeDtypeStruct((8, LANES), F32)],
        scratch_shapes=[pltpu.VMEM((1, LANES), F32)],
        compiler_params=_cparams(("arbitrary",)),
    )(h2, wr, br)


def _slot_kernel(idx_ref, pstart_ref, dest_ref, carry):
    i = pl.program_id(0)

    @pl.when(i == 0)
    def _():
        carry[...] = jnp.zeros_like(carry)

    tm = idx_ref.shape[0]
    idx = idx_ref[...]
    lane = lax.broadcasted_iota(jnp.int32, idx.shape, 1)
    oh1 = lane == idx[:, 0:1]
    oh2 = lane == idx[:, 1:2]
    oh = jnp.where(oh1, 1.0, jnp.where(oh2, 1.0, 0.0))
    ii = lax.broadcasted_iota(jnp.int32, (tm, tm), 0)
    jj = lax.broadcasted_iota(jnp.int32, (tm, tm), 1)
    earlier = jnp.dot(jnp.where(ii > jj, 1.0, 0.0).astype(BF16), oh.astype(BF16), preferred_element_type=F32)
    base = earlier + (carry[...] + pstart_ref[...])
    d1 = jnp.sum(jnp.where(oh1, base, 0.0), axis=-1, keepdims=True)
    d2 = jnp.sum(jnp.where(oh2, base, 0.0), axis=-1, keepdims=True)
    dest_ref[...] = jnp.where(lane == 0, d1, jnp.where(lane == 1, d2, 0.0)).astype(jnp.int32)
    carry[...] += jnp.sum(oh, axis=0, keepdims=True)


def _slots(idx, pstart, tm):
    N = idx.shape[0]
    return pl.pallas_call(
        _slot_kernel, name="slots", grid=(N // tm,),
        in_specs=[pl.BlockSpec((tm, LANES), lambda i: (i, 0)), pl.BlockSpec((1, LANES), lambda i: (0, 0))],
        out_specs=pl.BlockSpec((tm, LANES), lambda i: (i, 0)),
        out_shape=jax.ShapeDtypeStruct((N, LANES), jnp.int32),
        scratch_shapes=[pltpu.VMEM((1, LANES), F32)],
        compiler_params=_cparams(("arbitrary",)),
    )(idx, pstart)


DMA_UNROLL = 8


def _dispatch_kernel(plo_ref, phi_ref, dest_hbm, h_ref, xs_ref, dsm, zrow, isem, sem):
    tm = h_ref.shape[0]
    i = pl.program_id(0)
    cp = pltpu.make_async_copy(dest_hbm.at[i], dsm, isem)
    cp.start()

    @pl.when(i == 0)
    def _():
        zrow[...] = jnp.zeros_like(zrow)
        for e in range(plo_ref.shape[0]):
            def zero_slot(d, c):
                pltpu.make_async_copy(zrow, xs_ref.at[d], sem).start()
                return c

            def zero_done(d, c):
                pltpu.make_async_copy(zrow, xs_ref.at[0], sem).wait()
                return c

            lax.fori_loop(plo_ref[e], phi_ref[e], zero_slot, 0)
            lax.fori_loop(plo_ref[e], phi_ref[e], zero_done, 0)

    cp.wait()

    def issue(r, c):
        pltpu.make_async_copy(h_ref.at[r], xs_ref.at[dsm[r]], sem).start(priority=0)
        pltpu.make_async_copy(h_ref.at[r], xs_ref.at[dsm[tm + r]], sem).start(priority=1)
        return c

    lax.fori_loop(0, tm, issue, 0, unroll=DMA_UNROLL)
    for _ in range(2):
        pltpu.make_async_copy(h_ref, xs_ref.at[pl.ds(0, tm)], sem).wait()


def _dispatch(pad_lo, pad_hi, dest_tiles, h2, rows, tm):
    N, ns, _ = h2.shape
    return pl.pallas_call(
        _dispatch_kernel, name="dispatch",
        grid_spec=pltpu.PrefetchScalarGridSpec(
            num_scalar_prefetch=2, grid=(N // tm,),
            in_specs=[pl.BlockSpec(memory_space=pl.ANY),
                      pl.BlockSpec((tm, ns, LANES), lambda i, lo, hi: (i, 0, 0))],
            out_specs=pl.BlockSpec(memory_space=pl.ANY),
            scratch_shapes=[pltpu.SMEM((2 * tm,), jnp.int32), pltpu.VMEM((ns, LANES), F32),
                            pltpu.SemaphoreType.DMA, pltpu.SemaphoreType.DMA]),
        out_shape=jax.ShapeDtypeStruct((rows, ns, LANES), F32),
        compiler_params=_cparams(("arbitrary",)),
    )(pad_lo, pad_hi, dest_tiles, h2)


def _expert_kernel(be_ref, nu_ref, xs_ref, wg_ref, wu_ref, wd_ref, ys_ref):
    del be_ref
    used = pl.program_id(0) < nu_ref[0]

    @pl.when(used)
    def _():
        xb = _from_token_tiles(xs_ref).astype(BF16)
        g = jnp.dot(xb, wg_ref[...], preferred_element_type=F32)
        u = jnp.dot(xb, wu_ref[...], preferred_element_type=F32)
        ys_ref[...] = _dot(_silu(g) * u, wd_ref[...])

    @pl.when(jnp.logical_not(used))
    def _():
        ys_ref[...] = jnp.zeros_like(ys_ref)


def _experts(block_e, n_used, xs, wg, wu, wd, j):
    rows, ns, _ = xs.shape
    D = ns * LANES
    F = wg.shape[3]
    nblk = rows // MOE_ROWS

    def row_map(i, be, nu):
        return (jnp.minimum(i, nu[0] - 1), 0, 0)

    def w_map(i, be, nu):
        return (j, be[i], 0, 0)

    return pl.pallas_call(
        _expert_kernel, name="experts",
        grid_spec=pltpu.PrefetchScalarGridSpec(
            num_scalar_prefetch=2, grid=(nblk,),
            in_specs=[pl.BlockSpec((MOE_ROWS, ns, LANES), row_map),
                      pl.BlockSpec((None, None, D, F), w_map), pl.BlockSpec((None, None, D, F), w_map),
                      pl.BlockSpec((None, None, F, D), w_map)],
            out_specs=pl.BlockSpec((MOE_ROWS, D), lambda i, be, nu: (i, 0))),
        out_shape=jax.ShapeDtypeStruct((rows, D), F32),
        compiler_params=_cparams(("arbitrary",)),
    )(block_e, n_used, xs, wg, wu, wd)


def _combine_kernel(dest_hbm, ys_hbm, gate_ref, x1_ref, g2_ref, fg_ref, o_ref, dsm, b0, b1, isem, sem, *,
                    final_norm):
    bb, tt, D = x1_ref.shape
    tm = bb * tt
    i = pl.program_id(0) * pl.num_programs(1) + pl.program_id(1)
    cp = pltpu.make_async_copy(dest_hbm.at[i], dsm, isem)
    cp.start()
    cp.wait()

    def issue(r, c):
        pltpu.make_async_copy(ys_hbm.at[pl.ds(dsm[r], 1)], b0.at[pl.ds(r, 1)], sem).start(priority=0)
        pltpu.make_async_copy(ys_hbm.at[pl.ds(dsm[tm + r], 1)], b1.at[pl.ds(r, 1)], sem).start(priority=1)
        return c

    lax.fori_loop(0, tm, issue, 0, unroll=DMA_UNROLL)
    pltpu.make_async_copy(ys_hbm.at[pl.ds(0, tm)], b0, sem).wait()
    pltpu.make_async_copy(ys_hbm.at[pl.ds(0, tm)], b1, sem).wait()
    g = gate_ref[...]
    y = g[:, 0:1] * b0[...] + g[:, 1:2] * b1[...]
    x2 = x1_ref[...] + g2_ref[...] * y.reshape(bb, tt, D)
    if final_norm:
        x2 = x2 * lax.rsqrt(jnp.mean(x2 * x2, axis=-1, keepdims=True) + EPS) * fg_ref[...]
    o_ref[...] = x2


def _combine(dest_tiles, ys, gates, x1, mod, l, fgain, final_norm):
    B, T, D = x1.shape
    bb, tt = _tok_tiles(B, T)
    nt = T // tt
    tm = bb * tt
    ns = D // LANES
    return pl.pallas_call(
        functools.partial(_combine_kernel, final_norm=final_norm), name="combine", grid=(B // bb, nt),
        in_specs=[pl.BlockSpec(memory_space=pl.ANY), pl.BlockSpec(memory_space=pl.ANY),
                  pl.BlockSpec((tm, LANES), lambda b, t: (b * nt + t, 0)),
                  pl.BlockSpec((bb, tt, D), lambda b, t: (b, t, 0)),
                  _mod_spec(l, 5, bb, D), pl.BlockSpec((1, D), lambda b, t: (0, 0))],
        out_specs=pl.BlockSpec((bb, tt, D), lambda b, t: (b, t, 0)),
        out_shape=jax.ShapeDtypeStruct((B, T, D), F32),
        scratch_shapes=[pltpu.SMEM((2 * tm,), jnp.int32), pltpu.VMEM((tm, D), F32),
                        pltpu.VMEM((tm, D), F32), pltpu.SemaphoreType.DMA, pltpu.SemaphoreType.DMA],
        compiler_params=_cparams(("arbitrary", "arbitrary")),
    )(dest_tiles, ys, gates, x1, mod, fgain)


def _moe(h2, x1, mod, wr, br, wg, wu, wd, l, jl, fgain, final_norm):
    B, T, D = x1.shape
    bb, tt = _tok_tiles(B, T)
    tm = bb * tt
    N = B * T
    idx, gates, cnt = _router(h2, wr, br, tm, jl)
    counts = cnt[0, :N_EXPERTS].astype(jnp.int32)
    padded = (counts + MOE_ROWS - 1) // MOE_ROWS * MOE_ROWS
    pends = jnp.cumsum(padded)
    pstart = pends - padded
    nblk = -(-(2 * N) // MOE_ROWS) + N_EXPERTS
    starts = jnp.arange(nblk, dtype=jnp.int32) * MOE_ROWS
    block_e = jnp.minimum(jnp.sum((pends[None, :] <= starts[:, None]).astype(jnp.int32), axis=1), N_EXPERTS - 1)
    n_used = (pends[-1:] // MOE_ROWS).astype(jnp.int32)
    pstart_vec = jnp.zeros((1, LANES), F32).at[0, :N_EXPERTS].set(pstart.astype(F32))
    dest = _slots(idx, pstart_vec, tm)
    dest_tiles = dest[:, :2].reshape(N // tm, tm, 2).transpose(0, 2, 1).reshape(N // tm, 2 * tm)
    rows = nblk * MOE_ROWS
    pad_lo = jnp.concatenate([pstart + counts, pends[-1:]])
    pad_hi = jnp.concatenate([pends, jnp.full((1,), rows, jnp.int32)])
    xs = _dispatch(pad_lo, pad_hi, dest_tiles, h2, rows, tm)
    ys = _experts(block_e, n_used, xs, wg, wu, wd, jl)
    return _combine(dest_tiles, ys, gates, x1, mod, l, fgain, final_norm)


def _final_kernel(x_ref, g_ref, o_ref):
    x = x_ref[...]
    o_ref[...] = x * lax.rsqrt(jnp.mean(x * x, axis=-1, keepdims=True) + EPS) * g_ref[...]


def _final(x, g):
    B, T, D = x.shape
    bb, tt = _tok_tiles(B, T)
    return pl.pallas_call(
        _final_kernel, name="final", grid=(B // bb, T // tt),
        in_specs=[pl.BlockSpec((bb, tt, D), lambda b, t: (b, t, 0)), pl.BlockSpec((1, D), lambda b, t: (0, 0))],
        out_specs=pl.BlockSpec((bb, tt, D), lambda b, t: (b, t, 0)),
        out_shape=jax.ShapeDtypeStruct((B, T, D), F32),
        compiler_params=_cparams(("parallel", "parallel")),
    )(x, g)


def _trunk(x, mods, st, p):
    L = mods.shape[0]
    new = [[] for _ in range(5)]
    h0_all = st[4][:, :, None, :]
    nd_stack = jnp.zeros(st[2].shape, F32)
    for l in range(L):
        dense = l % 2 == 0
        j = l // 2
        u, qkv, z, lru, ba = _inproj(x, mods, p['norm1'], p['w_in'], p['w_in_tail'], l)
        ya, n_a = _mixa(u, st[0], p['conv_a_w'], p['conv_a_b'], p['ln_a_g'], p['ln_a_b'], l)
        yb, n_q, nd_stack = _gdn(qkv, z, ba, st[1], st[2], p['conv_qkv_w'], p['alog'], p['dtb'], p['gdn_norm'],
                                 nd_stack, l)
        yc, n_cl, n_h = _lru(lru, st[3], h0_all, p['conv_lru_w'], p['conv_lru_b'], p['lru_wa'], p['lru_b_a'],
                             p['lru_wi'], p['lru_b_i'], p['lru_lambda'], p['lru_norm'], l)
        x1, h2 = _outproj(ya, yb, yc, x, mods, p['norm2'], p['w_out'], l, tile_out=not dense)
        if dense:
            x = _ffn(h2, x1, mods, p['ffn_wg'], p['ffn_wu'], p['ffn_wd'], l, j)
        else:
            x = _moe(h2, x1, mods, p['router_w'], p['router_b'], p['moe_wg'], p['moe_wu'], p['moe_wd'], l, j,
                     p['final_norm'], l == L - 1)
        for lst, val in zip(new, (n_a, n_q, None, n_cl, n_h[:, 0, :])):
            lst.append(val)
    y = x if (L - 1) % 2 == 1 else _final(x, p['final_norm'])
    return y, [nd_stack if k == 2 else jnp.stack(v, axis=0) for k, v in enumerate(new)]


def kernel(x_prompt, x_sample, c_prompt, c_sample, state_conv_a, state_conv_qkv, state_delta, state_conv_lru, state_lru, w_ada, b_ada, norm1, norm2, w_in, w_out, conv_a_w, conv_a_b, ln_a_g, ln_a_b, conv_qkv_w, gdn_a_log, gdn_dt_bias, gdn_norm, conv_lru_w, conv_lru_b, lru_w_a, lru_b_a, lru_w_i, lru_b_i, lru_lambda, lru_norm, ffn_wg, ffn_wu, ffn_wd, router_w, router_b, moe_wg, moe_wu, moe_wd, final_norm):
    L, D = norm1.shape
    Bp, Bs = x_prompt.shape[0], x_sample.shape[0]
    nh = gdn_a_log.shape[1]

    def row(v):
        return v[:, None, :]

    def lanes(v):
        return jnp.zeros((L, 1, LANES), F32).at[:, 0, nh:2 * nh].set(v)

    def block_diag(w):
        n, k = w.shape[1], w.shape[2]
        eye = jnp.eye(n, dtype=w.dtype)
        return jnp.einsum('lnij,nm->lnimj', w, eye).reshape(L, n * k, n * k)

    w_in_tail = jnp.concatenate([w_in[:, :, _IN_HEAD + 2 * nh:], w_in[:, :, _IN_HEAD:_IN_HEAD + 2 * nh],
                                 jnp.zeros((L, D, LANES - 2 * nh), w_in.dtype)], axis=-1).astype(BF16)
    n_moe = router_w.shape[0]
    p = dict(
        norm1=row(norm1), norm2=row(norm2), w_in=w_in.astype(BF16), w_in_tail=w_in_tail,
        w_out=w_out.astype(BF16),
        conv_a_w=conv_a_w, conv_a_b=row(conv_a_b), ln_a_g=row(ln_a_g), ln_a_b=row(ln_a_b),
        conv_qkv_w=conv_qkv_w, alog=lanes(gdn_a_log), dtb=lanes(gdn_dt_bias), gdn_norm=row(gdn_norm),
        conv_lru_w=conv_lru_w, conv_lru_b=row(conv_lru_b), lru_wa=block_diag(lru_w_a), lru_b_a=row(lru_b_a),
        lru_wi=block_diag(lru_w_i), lru_b_i=row(lru_b_i), lru_lambda=row(lru_lambda), lru_norm=row(lru_norm),
        ffn_wg=ffn_wg.astype(BF16), ffn_wu=ffn_wu.astype(BF16), ffn_wd=ffn_wd.astype(BF16),
        router_w=jnp.concatenate([router_w, jnp.zeros((n_moe, D, LANES - N_EXPERTS), F32)], axis=-1),
        router_b=jnp.concatenate([router_b, jnp.full((n_moe, LANES - N_EXPERTS), NEG_BIG, F32)],
                                 axis=-1)[:, None, :],
        moe_wg=moe_wg.astype(BF16), moe_wu=moe_wu.astype(BF16), moe_wd=moe_wd.astype(BF16),
        final_norm=final_norm[None, :],
    )

    c_all = jnp.concatenate([c_prompt, c_sample], axis=0)
    mod = _ada(c_all, w_ada, b_ada)
    mod = mod.reshape(L, Bp + Bs, 6, D).transpose(0, 2, 1, 3)[:, :, :, None, :]
    mod_p, mod_s = mod[:, :, :Bp], mod[:, :, Bp:]

    zero_states = [jnp.zeros((L, Bp) + s.shape[2:], F32)
                   for s in (state_conv_a, state_conv_qkv, state_delta, state_conv_lru, state_lru)]
    y_p, st_p = _trunk(x_prompt, mod_p, zero_states, p)
    y_s, st_s = _trunk(x_sample, mod_s,
                       [state_conv_a, state_conv_qkv, state_delta, state_conv_lru, state_lru], p)
    return (y_p, y_s, *st_p, *st_s)
```

```python
import functools

import jax
import jax.numpy as jnp
from jax import lax
from jax.experimental import pallas as pl
from jax.experimental.pallas import tpu as pltpu

F32 = jnp.float32
BF16 = jnp.bfloat16
EPS = 1e-6

V7X_VMEM_LIMIT_BYTES = 56 * 1024 * 1024
LANES = 128
TOKEN_TILE = 512
MOE_ROWS = 512
N_EXPERTS = 8
GDN_CHUNK = 64
GDN_GROUP_CHUNKS = 2
LRU_C = 8.0
NEG_BIG = -1e30


def _cparams(sem):
    return pltpu.CompilerParams(dimension_semantics=sem, vmem_limit_bytes=V7X_VMEM_LIMIT_BYTES)


def _sigmoid(x):
    return jax.nn.sigmoid(x)


def _silu(x):
    return x * jax.nn.sigmoid(x)


def _softplus(x):
    return jnp.maximum(x, 0.0) + jnp.log1p(jnp.exp(-jnp.abs(x)))


def _dot(a, b):
    return jnp.dot(a.astype(BF16), b.astype(BF16), preferred_element_type=F32)


def _dot_nt(a, b):
    return lax.dot_general(a.astype(BF16), b.astype(BF16), (((1,), (1,)), ((), ())),
                           preferred_element_type=F32)


def _split2(x):
    hi = x.astype(BF16)
    lo = (x - hi.astype(F32)).astype(BF16)
    return hi, lo


def _split3(x):
    hi = x.astype(BF16)
    r = x - hi.astype(F32)
    mid = r.astype(BF16)
    lo = (r - mid.astype(F32)).astype(BF16)
    return hi, mid, lo


def _dot_hp(a, b):
    ah, al = _split2(a)
    bh, bl = _split2(b)
    d = functools.partial(jnp.dot, preferred_element_type=F32)
    return d(ah, bh) + (d(ah, bl) + d(al, bh))


def _dot_exact_lhs(a_bf16, b):
    d = functools.partial(jnp.dot, preferred_element_type=F32)
    h, m, l = _split3(b)
    return d(a_bf16, h) + (d(a_bf16, m) + d(a_bf16, l))


def _tok_tiles(B, T, target=TOKEN_TILE):
    if T >= target:
        assert T % target == 0
        return 1, target
    bb = max(1, min(B, target // T))
    assert B % bb == 0
    return bb, T


def _ada_kernel(c_ref, w_ref, b_ref, o_ref):
    c = c_ref[...]
    o_ref[...] = _dot(_silu(c), w_ref[...]) + b_ref[...]


def _ada(c_all, w_ada, b_ada):
    L, D, D6 = w_ada.shape
    Bc = c_all.shape[0]
    tn = 1536
    return pl.pallas_call(
        _ada_kernel, name="ada", grid=(L, D6 // tn),
        in_specs=[pl.BlockSpec((Bc, D), lambda l, j: (0, 0)),
                  pl.BlockSpec((None, D, tn), lambda l, j: (l, 0, j)),
                  pl.BlockSpec((None, 1, tn), lambda l, j: (l, 0, j))],
        out_specs=pl.BlockSpec((None, Bc, tn), lambda l, j: (l, 0, j)),
        out_shape=jax.ShapeDtypeStruct((L, Bc, D6), F32),
        compiler_params=_cparams(("parallel", "parallel")),
    )(c_all, w_ada, b_ada.reshape(L, 1, D6))


def _mod_spec(l, j, bb, D):
    return pl.BlockSpec((None, None, bb, 1, D), lambda b, *_: (l, j, b, 0, 0))


def _layer_spec(l, shape):
    zeros = (0,) * len(shape)
    return pl.BlockSpec((None,) + tuple(shape), lambda *_: (l,) + zeros)


def _state_spec(l, bb, shape):
    zeros = (0,) * len(shape)
    return pl.BlockSpec((None, bb) + tuple(shape), lambda b, *_: (l, b) + zeros)


_IN_AG = (0, 512)
_IN_QKV = (512, 2048)
_IN_Z = (2048, 2560)
_IN_HEAD = 2560
_TAIL_LRU = (0, 512)
_TAIL_BA = (512, 640)
_IN_TAIL = 640


def _inproj_kernel(x_ref, sc_ref, sh_ref, g_ref, w_ref, wt_ref, u_ref, qkv_ref, z_ref, lru_ref, ba_ref):
    bb, tt, D = x_ref.shape
    x = x_ref[...]
    h = x * lax.rsqrt(jnp.mean(x * x, axis=-1, keepdims=True) + EPS) * g_ref[...]
    h = h * (1.0 + sc_ref[...]) + sh_ref[...]
    hb = h.reshape(bb * tt, D).astype(BF16)

    def seg(lo_hi, ref=w_ref):
        return jnp.dot(hb, ref[:, lo_hi[0]:lo_hi[1]], preferred_element_type=F32)

    ag = seg(_IN_AG)
    u_ref[...] = (ag[:, :256] * _sigmoid(ag[:, 256:])).reshape(bb, tt, 256)
    qkv_ref[...] = seg(_IN_QKV).reshape(bb, tt, 1536)
    z_ref[...] = seg(_IN_Z).reshape(bb, tt, 512)
    lru_ref[...] = seg(_TAIL_LRU, wt_ref).reshape(bb, tt, 512)
    ba_ref[...] = seg(_TAIL_BA, wt_ref).reshape(bb, tt, LANES)


def _inproj(x, mod, gain, w, wt, l):
    B, T, D = x.shape
    bb, tt = _tok_tiles(B, T)
    widths = (256, 1536, 512, 512, LANES)
    return pl.pallas_call(
        _inproj_kernel, name="inproj", grid=(B // bb, T // tt),
        in_specs=[pl.BlockSpec((bb, tt, D), lambda b, t: (b, t, 0)),
                  _mod_spec(l, 1, bb, D), _mod_spec(l, 0, bb, D),
                  _layer_spec(l, (1, D)), _layer_spec(l, (D, _IN_HEAD)), _layer_spec(l, (D, _IN_TAIL))],
        out_specs=[pl.BlockSpec((bb, tt, n), lambda b, t: (b, t, 0)) for n in widths],
        out_shape=[jax.ShapeDtypeStruct((B, T, n), F32) for n in widths],
        compiler_params=_cparams(("parallel", "parallel")),
    )(x, mod, mod, gain, w, wt)


def _mixa_kernel(u_ref, buf_ref, w_ref, cb_ref, lg_ref, lb_ref, y_ref, ns_ref, xp_ref):
    bb, tc, ch = u_ref.shape
    width = w_ref.shape[0]
    t = pl.program_id(1)

    @pl.when(t == 0)
    def _():
        xp_ref[:, 2:32, :] = buf_ref[...]

    @pl.when(t > 0)
    def _():
        xp_ref[:, 0:32, :] = xp_ref[:, tc:tc + 32, :]

    xp_ref[:, 32:32 + tc, :] = u_ref[...]
    ns_ref[...] = xp_ref[:, tc + 2:tc + 32, :]

    rs = min(tc, 64)
    win = rs + 32
    for b in range(bb):
        for r0 in range(0, tc, rs):
            w0 = xp_ref[b, r0:r0 + win, :]
            shifted = [w0] + [pltpu.roll(w0, win - r, 0) for r in range(1, 8)]
            acc = jnp.zeros((rs, ch), F32) + cb_ref[...]
            for j in range(width):
                a, r = divmod(2 + j, 8)
                acc = acc + w_ref[j:j + 1, :] * shifted[r][8 * a:8 * a + rs, :]
            mu = jnp.mean(acc, axis=-1, keepdims=True)
            d = acc - mu
            var = jnp.mean(d * d, axis=-1, keepdims=True)
            yn = d * lax.rsqrt(var + EPS) * lg_ref[...] + lb_ref[...]
            y_ref[b, r0:r0 + rs, :] = _silu(yn)


def _mixa(u, buf, w, cb, lg, lb, l):
    B, T, ch = u.shape
    if T >= 512:
        bb, tc = 1, 512
    elif T >= 256:
        bb, tc = 1, 256
    else:
        bb, tc = min(B, 8), T
    assert T % tc == 0 and B % bb == 0
    nb = buf.shape[2]
    vec = _layer_spec(l, (1, ch))
    return pl.pallas_call(
        _mixa_kernel, name="mixa", grid=(B // bb, T // tc),
        in_specs=[pl.BlockSpec((bb, tc, ch), lambda b, t: (b, t, 0)),
                  _state_spec(l, bb, (nb, ch)),
                  _layer_spec(l, w.shape[1:]), vec, vec, vec],
        out_specs=[pl.BlockSpec((bb, tc, ch), lambda b, t: (b, t, 0)),
                   pl.BlockSpec((bb, nb, ch), lambda b, t: (b, 0, 0))],
        out_shape=[jax.ShapeDtypeStruct((B, T, ch), F32), jax.ShapeDtypeStruct((B, nb, ch), F32)],
        scratch_shapes=[pltpu.VMEM((bb, 32 + tc, ch), F32)],
        compiler_params=_cparams(("parallel", "arbitrary")),
    )(u, buf, w, cb, lg, lb)


def _gdn_kernel(qkv_ref, z_ref, ba_ref, sq_ref, sd_ref, cw_ref, alog_ref, dtb_ref, gn_ref, nds_ref,
                y_ref, nq_ref, nd_ref, xp_ref, qn_ref, bg_ref, s_ref, *, C):
    del nds_ref
    bb, tt, width = qkv_ref.shape
    nh = sd_ref.shape[1]
    dk = sd_ref.shape[2]
    ntap = cw_ref.shape[0]
    t = pl.program_id(1)

    @pl.when(t == 0)
    def _():
        xp_ref[:, 5:8, :] = sq_ref[...]
        s_ref[...] = sd_ref[...]

    @pl.when(t > 0)
    def _():
        xp_ref[:, 5:8, :] = xp_ref[:, tt + 5:tt + 8, :]

    xp_ref[:, 8:8 + tt, :] = qkv_ref[...]
    nq_ref[...] = xp_ref[:, tt + 5:tt + 8, :]

    rs = min(tt, 128)
    win = rs + 8
    for b in range(bb):
        for sl in range(width // dk):
            lo = sl * dk
            for r0 in range(0, tt, rs):
                w0 = xp_ref[b, r0:r0 + win, lo:lo + dk]
                acc = cw_ref[ntap - 1:ntap, lo:lo + dk] * w0[8:8 + rs, :]
                for j in range(ntap - 1):
                    acc = acc + cw_ref[j:j + 1, lo:lo + dk] * pltpu.roll(w0, win - 5 - j, 0)[:rs, :]
                s = _silu(acc)
                if sl < 2 * nh:
                    s = s * lax.rsqrt(jnp.sum(s * s, axis=-1, keepdims=True) + EPS)
                if sl < nh:
                    s = s * (dk ** -0.5)
                qn_ref[b, r0:r0 + rs, lo:lo + dk] = s

    ba = ba_ref[...]
    lane3 = lax.broadcasted_iota(jnp.int32, ba.shape, 2)
    gdec = -jnp.exp(alog_ref[...]) * _softplus(ba + dtb_ref[...])
    bg_ref[...] = jnp.where(lane3 < nh, _sigmoid(ba), jnp.where(lane3 < 2 * nh, gdec, 0.0))

    ii = lax.broadcasted_iota(jnp.int32, (C, C), 0)
    jj = lax.broadcasted_iota(jnp.int32, (C, C), 1)
    tri_incl = jnp.where(ii >= jj, 1.0, 0.0).astype(BF16)
    lane_c = lax.broadcasted_iota(jnp.int32, (C, LANES), 1)
    nsteps = C.bit_length() - 1
    assert (1 << nsteps) == C
    ident = jnp.where(ii == jj, 1.0, 0.0)
    pair_mask = []
    for s in range(nsteps):
        same_big = jnp.right_shift(ii, s + 1) == jnp.right_shift(jj, s + 1)
        diff_small = jnp.right_shift(ii, s) != jnp.right_shift(jj, s)
        pair_mask.append((same_big, diff_small))

    def pair_part(s, m):
        return jnp.where(pair_mask[s][0], jnp.where(pair_mask[s][1], m, 0.0), 0.0)

    nc = tt // C
    state = {(b, h): s_ref[b, h] for b in range(bb) for h in range(nh)}
    for g0 in range(0, nc, GDN_GROUP_CHUNKS):
        cgrp = range(g0, min(g0 + GDN_GROUP_CHUNKS, nc))
        probs = [(b, c, h) for b in range(bb) for c in cgrp for h in range(nh)]
        bgc, gcs, gct = {}, {}, {}
        for b in range(bb):
            for c in cgrp:
                bgc[b, c] = bg_ref[b, c * C:(c + 1) * C, :]
                gcs[b, c] = _dot_exact_lhs(tri_incl, jnp.where(lane_c < nh, 0.0, bgc[b, c]))
        for key in gcs:
            gct[key] = gcs[key].T

        def rows(b, c, slab):
            return qn_ref[b, c * C:(c + 1) * C, slab * dk:(slab + 1) * dk]

        q = {p: rows(p[0], p[1], p[2]) for p in probs}
        k = {p: rows(p[0], p[1], nh + p[2]) for p in probs}
        gcol = {(b, c, h): gcs[b, c][:, nh + h:nh + h + 1] for b, c, h in probs}
        kb = {(b, c, h): k[b, c, h] * bgc[b, c][:, h:h + 1] for b, c, h in probs}
        a2 = {p: _dot_nt(jnp.concatenate([kb[p], q[p]], axis=0), k[p]) for p in probs}
        decay = {(b, c, h): jnp.exp(jnp.where(ii >= jj, gcol[b, c, h] - gct[b, c][nh + h:nh + h + 1, :], -jnp.inf))
                 for b, c, h in probs}
        low = {p: jnp.where(ii > jj, a2[p][:C] * decay[p], 0.0) for p in probs}
        tinv = {p: ident - pair_part(0, low[p]) for p in probs}
        for lvl in range(1, nsteps):
            half = {p: _dot(tinv[p], pair_part(lvl, low[p])) for p in probs}
            tinv = {p: tinv[p] - _dot(half[p], tinv[p]) for p in probs}
        rhs = {(b, c, h): jnp.concatenate([rows(b, c, 2 * nh + h) * bgc[b, c][:, h:h + 1],
                                            kb[b, c, h] * jnp.exp(gcol[b, c, h])], axis=1) for b, c, h in probs}
        uw = {p: _dot(tinv[p], rhs[p]) for p in probs}

        for c in cgrp:
            cur = [(b, c, h) for b in range(bb) for h in range(nh)]
            wq = {p: _dot(jnp.concatenate([uw[p][:, dk:], q[p] * jnp.exp(gcol[p])], axis=0), state[p[0], p[2]])
                  for p in cur}
            v_new = {p: uw[p][:, :dk] - wq[p][:C] for p in cur}
            o = {p: wq[p][C:] + _dot(a2[p][C:] * decay[p], v_new[p]) for p in cur}
            for b, _, h in cur:
                p = (b, c, h)
                gl = gcs[b, c][C - 1:C, nh + h:nh + h + 1]
                kdec = k[p] * jnp.exp(gl - gcol[p])
                upd = lax.dot_general(kdec.astype(BF16), v_new[p].astype(BF16), (((0,), (0,)), ((), ())),
                                      preferred_element_type=F32)
                state[b, h] = state[b, h] * jnp.exp(gl) + upd
            for b, _, h in cur:
                p = (b, c, h)
                on = o[p] * lax.rsqrt(jnp.mean(o[p] * o[p], axis=-1, keepdims=True) + EPS) * gn_ref[...]
                zz = z_ref[b, c * C:(c + 1) * C, h * dk:(h + 1) * dk]
                y_ref[b, c * C:(c + 1) * C, h * dk:(h + 1) * dk] = on * _silu(zz)
    for (b, h), val in state.items():
        s_ref[b, h] = val
    nd_ref[...] = s_ref[...]


def _gdn(qkv, z, ba, s_qkv, s_delta, cw, alog_vec, dtb_vec, gn, nd_stack, l):
    B, T, width = qkv.shape
    nh, dk = s_delta.shape[2], s_delta.shape[3]
    C = min(GDN_CHUNK, T)
    if T > C:
        bb = min(B, 2)
        tt = min(T, 8 * C // bb)
    else:
        bb, tt = min(B, 8), T
    assert T % tt == 0 and tt % C == 0 and B % bb == 0
    vec = _layer_spec(l, (1, LANES))
    kern = functools.partial(_gdn_kernel, C=C)
    return pl.pallas_call(
        kern, name="gdn", grid=(B // bb, T // tt),
        in_specs=[pl.BlockSpec((bb, tt, width), lambda b, t: (b, t, 0)),
                  pl.BlockSpec((bb, tt, nh * dk), lambda b, t: (b, t, 0)),
                  pl.BlockSpec((bb, tt, LANES), lambda b, t: (b, t, 0)),
                  _state_spec(l, bb, (3, width)),
                  _state_spec(l, bb, (nh, dk, dk)),
                  _layer_spec(l, cw.shape[1:]), vec, vec, vec, pl.BlockSpec(memory_space=pl.ANY)],
        out_specs=[pl.BlockSpec((bb, tt, nh * dk), lambda b, t: (b, t, 0)),
                   pl.BlockSpec((bb, 3, width), lambda b, t: (b, 0, 0)),
                   _state_spec(l, bb, (nh, dk, dk))],
        out_shape=[jax.ShapeDtypeStruct((B, T, nh * dk), F32),
                   jax.ShapeDtypeStruct((B, 3, width), F32),
                   jax.ShapeDtypeStruct(nd_stack.shape, F32)],
        input_output_aliases={9: 2},
        scratch_shapes=[pltpu.VMEM((bb, 8 + tt, width), F32), pltpu.VMEM((bb, tt, width), F32),
                        pltpu.VMEM((bb, tt, LANES), F32), pltpu.VMEM((bb, nh, dk, dk), F32)],
        compiler_params=_cparams(("parallel", "arbitrary")),
    )(qkv, z, ba, s_qkv, s_delta, cw, alog_vec, dtb_vec, gn, nd_stack)


def _lru_kernel(lru_ref, cs_ref, h0_ref, cw_ref, cb_ref, wa_ref, ba_ref, wi_ref, bi_ref, lam_ref, ng_ref,
                y_ref, ncs_ref, nh_ref, xp_ref, sa_ref, sb_ref, h_ref):
    bb, tc, two_ch = lru_ref.shape
    ch = two_ch // 2
    ntap = cw_ref.shape[0]
    t = pl.program_id(1)

    @pl.when(t == 0)
    def _():
        xp_ref[:, 5:8, :] = cs_ref[...]
        h_ref[...] = h0_ref[...]

    @pl.when(t > 0)
    def _():
        xp_ref[:, 5:8, :] = xp_ref[:, tc + 5:tc + 8, :]

    xp_ref[:, 8:8 + tc, :] = lru_ref[:, :, 0:ch]
    ncs_ref[...] = xp_ref[:, tc + 5:tc + 8, :]

    xc = jnp.zeros((bb, tc, ch), F32) + cb_ref[...]
    for j in range(ntap):
        xc = xc + cw_ref[j:j + 1, :] * xp_ref[:, 5 + j:5 + j + tc, :]
    x2 = xc.reshape(bb * tc, ch)
    r = _sigmoid(_dot_hp(x2, wa_ref[...]) + ba_ref[...])
    i = _sigmoid(_dot_hp(x2, wi_ref[...]) + bi_ref[...])
    log_a = (-LRU_C * r) * _softplus(-lam_ref[...])
    a = jnp.exp(log_a)
    bv = jnp.sqrt(1.0 - jnp.exp(2.0 * log_a)) * (i * x2)
    a3 = a.reshape(bb, tc, ch)
    b3 = bv.reshape(bb, tc, ch)

    sa_ref[:, 0:tc, :] = jnp.ones((bb, tc, ch), F32)
    sb_ref[:, 0:tc, :] = jnp.zeros((bb, tc, ch), F32)
    d = 1
    while d < tc:
        sa_ref[:, tc:2 * tc, :] = a3
        sb_ref[:, tc:2 * tc, :] = b3
        ap = sa_ref[:, tc - d:2 * tc - d, :]
        bp = sb_ref[:, tc - d:2 * tc - d, :]
        b3 = a3 * bp + b3
        a3 = a3 * ap
        d *= 2
    h = a3 * h_ref[...] + b3
    h_last = h[:, tc - 1:tc, :]
    h_ref[...] = h_last
    nh_ref[...] = h_last

    y = h * jax.nn.gelu(lru_ref[:, :, ch:two_ch])
    y_ref[...] = y * lax.rsqrt(jnp.mean(y * y, axis=-1, keepdims=True) + EPS) * ng_ref[...]


def _lru(lru, cs, h0, cw, cb, wa, ba, wi, bi, lam, ng, l):
    B, T, two_ch = lru.shape
    ch = two_ch // 2
    if T >= 512:
        bb, tc = 1, 512
    elif T >= 256:
        bb, tc = 1, 256
    else:
        bb, tc = min(B, 16), T
    assert T % tc == 0 and B % bb == 0
    vec = _layer_spec(l, (1, ch))
    mat = _layer_spec(l, (ch, ch))
    return pl.pallas_call(
        _lru_kernel, name="lru", grid=(B // bb, T // tc),
        in_specs=[pl.BlockSpec((bb, tc, two_ch), lambda b, t: (b, t, 0)),
                  _state_spec(l, bb, (3, ch)),
                  _state_spec(l, bb, (1, ch)),
                  _layer_spec(l, cw.shape[1:]), vec, mat, vec, mat, vec, vec, vec],
        out_specs=[pl.BlockSpec((bb, tc, ch), lambda b, t: (b, t, 0)),
                   pl.BlockSpec((bb, 3, ch), lambda b, t: (b, 0, 0)),
                   pl.BlockSpec((bb, 1, ch), lambda b, t: (b, 0, 0))],
        out_shape=[jax.ShapeDtypeStruct((B, T, ch), F32), jax.ShapeDtypeStruct((B, 3, ch), F32),
                   jax.ShapeDtypeStruct((B, 1, ch), F32)],
        scratch_shapes=[pltpu.VMEM((bb, 8 + tc, ch), F32), pltpu.VMEM((bb, 2 * tc, ch), F32),
                        pltpu.VMEM((bb, 2 * tc, ch), F32), pltpu.VMEM((bb, 1, ch), F32)],
        compiler_params=_cparams(("parallel", "arbitrary")),
    )(lru, cs, h0, cw, cb, wa, ba, wi, bi, lam, ng)


def _outproj_kernel(ya_ref, yb_ref, yc_ref, x_ref, g1_ref, sc_ref, sh_ref, n2_ref, w_ref, x1_ref, h2_ref):
    bb, tt, D = x_ref.shape
    m = bb * tt
    na, nb = ya_ref.shape[2], yb_ref.shape[2]
    mix = (_dot(ya_ref[...].reshape(m, na), w_ref[0:na, :])
           + _dot(yb_ref[...].reshape(m, nb), w_ref[na:na + nb, :])
           + _dot(yc_ref[...].reshape(m, D - na - nb), w_ref[na + nb:D, :]))
    x1 = x_ref[...] + g1_ref[...] * mix.reshape(bb, tt, D)
    x1_ref[...] = x1
    h = x1 * lax.rsqrt(jnp.mean(x1 * x1, axis=-1, keepdims=True) + EPS) * n2_ref[...]
    h = (h * (1.0 + sc_ref[...]) + sh_ref[...]).reshape(m, D)
    if len(h2_ref.shape) == 2:
        h2_ref[...] = h
    else:
        _to_token_tiles(h, h2_ref)


def _to_token_tiles(val, ref):
    for s in range(ref.shape[1]):
        ref[:, s, :] = val[:, s * LANES:(s + 1) * LANES]


def _from_token_tiles(ref):
    return jnp.concatenate([ref[:, s, :] for s in range(ref.shape[1])], axis=1)


def _outproj(ya, yb, yc, x, mod, gain2, w, l, tile_out):
    B, T, D = x.shape
    bb, tt = _tok_tiles(B, T)
    nt = T // tt

    def tok(n):
        return pl.BlockSpec((bb, tt, n), lambda b, t: (b, t, 0))

    if tile_out:
        h2_spec = pl.BlockSpec((bb * tt, D // LANES, LANES), lambda b, t: (b * nt + t, 0, 0))
        h2_shape = jax.ShapeDtypeStruct((B * T, D // LANES, LANES), F32)
    else:
        h2_spec = pl.BlockSpec((bb * tt, D), lambda b, t: (b * nt + t, 0))
        h2_shape = jax.ShapeDtypeStruct((B * T, D), F32)
    return pl.pallas_call(
        _outproj_kernel, name="outproj", grid=(B // bb, nt),
        in_specs=[tok(ya.shape[2]), tok(yb.shape[2]), tok(yc.shape[2]), tok(D),
                  _mod_spec(l, 2, bb, D), _mod_spec(l, 4, bb, D), _mod_spec(l, 3, bb, D),
                  _layer_spec(l, (1, D)), _layer_spec(l, (D, D))],
        out_specs=[tok(D), h2_spec],
        out_shape=[jax.ShapeDtypeStruct((B, T, D), F32), h2_shape],
        compiler_params=_cparams(("parallel", "parallel")),
    )(ya, yb, yc, x, mod, mod, mod, gain2, w)


def _ffn_kernel(h_ref, x1_ref, g2_ref, wg_ref, wu_ref, wd_ref, o_ref, acc_ref):
    bb, tt, D = x1_ref.shape
    j = pl.program_id(2)

    @pl.when(j == 0)
    def _():
        acc_ref[...] = jnp.zeros_like(acc_ref)

    hb = h_ref[...].astype(BF16)
    g = jnp.dot(hb, wg_ref[...], preferred_element_type=F32)
    u = jnp.dot(hb, wu_ref[...], preferred_element_type=F32)
    acc_ref[...] += _dot(_silu(g) * u, wd_ref[...])

    @pl.when(j == pl.num_programs(2) - 1)
    def _():
        o_ref[...] = x1_ref[...] + g2_ref[...] * acc_ref[...].reshape(bb, tt, D)


def _ffn(h2, x1, mod, wg, wu, wd, l, jl):
    B, T, D = x1.shape
    bb, tt = _tok_tiles(B, T)
    nt = T // tt
    F = wg.shape[2]
    tf = F // 2
    assert tf % LANES == 0
    return pl.pallas_call(
        _ffn_kernel, name="ffn", grid=(B // bb, nt, F // tf),
        in_specs=[pl.BlockSpec((bb * tt, D), lambda b, t, j: (b * nt + t, 0)),
                  pl.BlockSpec((bb, tt, D), lambda b, t, j: (b, t, 0)),
                  _mod_spec(l, 5, bb, D),
                  pl.BlockSpec((None, D, tf), lambda b, t, j: (jl, 0, j)),
                  pl.BlockSpec((None, D, tf), lambda b, t, j: (jl, 0, j)),
                  pl.BlockSpec((None, tf, D), lambda b, t, j: (jl, j, 0))],
        out_specs=pl.BlockSpec((bb, tt, D), lambda b, t, j: (b, t, 0)),
        out_shape=jax.ShapeDtypeStruct((B, T, D), F32),
        scratch_shapes=[pltpu.VMEM((bb * tt, D), F32)],
        compiler_params=_cparams(("parallel", "parallel", "arbitrary")),
    )(h2, x1, mod, wg, wu, wd)


def _router_kernel(h_ref, wr_ref, br_ref, idx_ref, gate_ref, cnt_ref, cnt_acc):
    i = pl.program_id(0)

    @pl.when(i == 0)
    def _():
        cnt_acc[...] = jnp.zeros_like(cnt_acc)

    logits = _dot_hp(_from_token_tiles(h_ref), wr_ref[...]) + br_ref[...]
    lane = lax.broadcasted_iota(jnp.int32, logits.shape, 1).astype(F32)
    m1 = jnp.max(logits, axis=-1, keepdims=True)
    i1 = jnp.min(jnp.where(logits == m1, lane, float(LANES)), axis=-1, keepdims=True)
    rest = jnp.where(lane == i1, NEG_BIG, logits)
    m2 = jnp.max(rest, axis=-1, keepdims=True)
    i2 = jnp.min(jnp.where(rest == m2, lane, float(LANES)), axis=-1, keepdims=True)
    e = jnp.exp(m2 - m1)
    g1 = 1.0 / (1.0 + e)
    g2 = e / (1.0 + e)
    idx_ref[...] = jnp.where(lane == 0.0, i1, jnp.where(lane == 1.0, i2, 0.0)).astype(jnp.int32)
    gate_ref[...] = jnp.where(lane == 0.0, g1, jnp.where(lane == 1.0, g2, 0.0))
    oh = jnp.where(lane == i1, 1.0, jnp.where(lane == i2, 1.0, 0.0))
    cnt_acc[...] += jnp.sum(oh, axis=0, keepdims=True)
    cnt_ref[...] = jnp.broadcast_to(cnt_acc[...], cnt_ref.shape)


def _router(h2, wr, br, tm, jl):
    N, ns, _ = h2.shape
    D = ns * LANES
    return pl.pallas_call(
        _router_kernel, name="router", grid=(N // tm,),
        in_specs=[pl.BlockSpec((tm, ns, LANES), lambda i: (i, 0, 0)),
                  _layer_spec(jl, (D, LANES)), _layer_spec(jl, (1, LANES))],
        out_specs=[pl.BlockSpec((tm, LANES), lambda i: (i, 0)),
                   pl.BlockSpec((tm, LANES), lambda i: (i, 0)),
                   pl.BlockSpec((8, LANES), lambda i: (0, 0))],
        out_shape=[jax.ShapeDtypeStruct((N, LANES), jnp.int32), jax.ShapeDtypeStruct((N, LANES), F32),
                   jax.ShapeDtypeStruct((8, LANES), F32)],
        scratch_shapes=[pltpu.VMEM((1, LANES), F32)],
        compiler_params=_cparams(("arbitrary",)),
    )(h2, wr, br)


def _slot_kernel(idx_ref, pstart_ref, dest_ref, carry):
    i = pl.program_id(0)

    @pl.when(i == 0)
    def _():
        carry[...] = jnp.zeros_like(carry)

    tm = idx_ref.shape[0]
    idx = idx_ref[...]
    lane = lax.broadcasted_iota(jnp.int32, idx.shape, 1)
    oh1 = lane == idx[:, 0:1]
    oh2 = lane == idx[:, 1:2]
    oh = jnp.where(oh1, 1.0, jnp.where(oh2, 1.0, 0.0))
    ii = lax.broadcasted_iota(jnp.int32, (tm, tm), 0)
    jj = lax.broadcasted_iota(jnp.int32, (tm, tm), 1)
    earlier = jnp.dot(jnp.where(ii > jj, 1.0, 0.0).astype(BF16), oh.astype(BF16), preferred_element_type=F32)
    base = earlier + (carry[...] + pstart_ref[...])
    d1 = jnp.sum(jnp.where(oh1, base, 0.0), axis=-1, keepdims=True)
    d2 = jnp.sum(jnp.where(oh2, base, 0.0), axis=-1, keepdims=True)
    dest_ref[...] = jnp.where(lane == 0, d1, jnp.where(lane == 1, d2, 0.0)).astype(jnp.int32)
    carry[...] += jnp.sum(oh, axis=0, keepdims=True)


def _slots(idx, pstart, tm):
    N = idx.shape[0]
    return pl.pallas_call(
        _slot_kernel, name="slots", grid=(N // tm,),
        in_specs=[pl.BlockSpec((tm, LANES), lambda i: (i, 0)), pl.BlockSpec((1, LANES), lambda i: (0, 0))],
        out_specs=pl.BlockSpec((tm, LANES), lambda i: (i, 0)),
        out_shape=jax.ShapeDtypeStruct((N, LANES), jnp.int32),
        scratch_shapes=[pltpu.VMEM((1, LANES), F32)],
        compiler_params=_cparams(("arbitrary",)),
    )(idx, pstart)


DMA_UNROLL = 8


def _dispatch_kernel(plo_ref, phi_ref, dest_hbm, h_ref, xs_ref, dsm, zrow, isem, sem):
    tm = h_ref.shape[0]
    i = pl.program_id(0)
    cp = pltpu.make_async_copy(dest_hbm.at[i], dsm, isem)
    cp.start()

    @pl.when(i == 0)
    def _():
        zrow[...] = jnp.zeros_like(zrow)
        for e in range(plo_ref.shape[0]):
            def zero_slot(d, c):
                pltpu.make_async_copy(zrow, xs_ref.at[d], sem).start()
                return c

            def zero_done(d, c):
                pltpu.make_async_copy(zrow, xs_ref.at[0], sem).wait()
                return c

            lax.fori_loop(plo_ref[e], phi_ref[e], zero_slot, 0)
            lax.fori_loop(plo_ref[e], phi_ref[e], zero_done, 0)

    cp.wait()

    def issue(r, c):
        pltpu.make_async_copy(h_ref.at[r], xs_ref.at[dsm[r]], sem).start(priority=0)
        pltpu.make_async_copy(h_ref.at[r], xs_ref.at[dsm[tm + r]], sem).start(priority=1)
        return c

    lax.fori_loop(0, tm, issue, 0, unroll=DMA_UNROLL)
    for _ in range(2):
        pltpu.make_async_copy(h_ref, xs_ref.at[pl.ds(0, tm)], sem).wait()


def _dispatch(pad_lo, pad_hi, dest_tiles, h2, rows, tm):
    N, ns, _ = h2.shape
    return pl.pallas_call(
        _dispatch_kernel, name="dispatch",
        grid_spec=pltpu.PrefetchScalarGridSpec(
            num_scalar_prefetch=2, grid=(N // tm,),
            in_specs=[pl.BlockSpec(memory_space=pl.ANY),
                      pl.BlockSpec((tm, ns, LANES), lambda i, lo, hi: (i, 0, 0))],
            out_specs=pl.BlockSpec(memory_space=pl.ANY),
            scratch_shapes=[pltpu.SMEM((2 * tm,), jnp.int32), pltpu.VMEM((ns, LANES), F32),
                            pltpu.SemaphoreType.DMA, pltpu.SemaphoreType.DMA]),
        out_shape=jax.ShapeDtypeStruct((rows, ns, LANES), F32),
        compiler_params=_cparams(("arbitrary",)),
    )(pad_lo, pad_hi, dest_tiles, h2)


def _expert_kernel(be_ref, nu_ref, xs_ref, wg_ref, wu_ref, wd_ref, ys_ref):
    del be_ref
    used = pl.program_id(0) < nu_ref[0]

    @pl.when(used)
    def _():
        xb = _from_token_tiles(xs_ref).astype(BF16)
        g = jnp.dot(xb, wg_ref[...], preferred_element_type=F32)
        u = jnp.dot(xb, wu_ref[...], preferred_element_type=F32)
        ys_ref[...] = _dot(_silu(g) * u, wd_ref[...])

    @pl.when(jnp.logical_not(used))
    def _():
        ys_ref[...] = jnp.zeros_like(ys_ref)


def _experts(block_e, n_used, xs, wg, wu, wd, j):
    rows, ns, _ = xs.shape
    D = ns * LANES
    F = wg.shape[3]
    nblk = rows // MOE_ROWS

    def row_map(i, be, nu):
        return (jnp.minimum(i, nu[0] - 1), 0, 0)

    def w_map(i, be, nu):
        return (j, be[i], 0, 0)

    return pl.pallas_call(
        _expert_kernel, name="experts",
        grid_spec=pltpu.PrefetchScalarGridSpec(
            num_scalar_prefetch=2, grid=(nblk,),
            in_specs=[pl.BlockSpec((MOE_ROWS, ns, LANES), row_map),
                      pl.BlockSpec((None, None, D, F), w_map), pl.BlockSpec((None, None, D, F), w_map),
                      pl.BlockSpec((None, None, F, D), w_map)],
            out_specs=pl.BlockSpec((MOE_ROWS, D), lambda i, be, nu: (i, 0))),
        out_shape=jax.ShapeDtypeStruct((rows, D), F32),
        compiler_params=_cparams(("arbitrary",)),
    )(block_e, n_used, xs, wg, wu, wd)


def _combine_kernel(dest_hbm, ys_hbm, gate_ref, x1_ref, g2_ref, fg_ref, o_ref, dsm, b0, b1, isem, sem, *,
                    final_norm):
    bb, tt, D = x1_ref.shape
    tm = bb * tt
    i = pl.program_id(0) * pl.num_programs(1) + pl.program_id(1)
    cp = pltpu.make_async_copy(dest_hbm.at[i], dsm, isem)
    cp.start()
    cp.wait()

    def issue(r, c):
        pltpu.make_async_copy(ys_hbm.at[pl.ds(dsm[r], 1)], b0.at[pl.ds(r, 1)], sem).start(priority=0)
        pltpu.make_async_copy(ys_hbm.at[pl.ds(dsm[tm + r], 1)], b1.at[pl.ds(r, 1)], sem).start(priority=1)
        return c

    lax.fori_loop(0, tm, issue, 0, unroll=DMA_UNROLL)
    pltpu.make_async_copy(ys_hbm.at[pl.ds(0, tm)], b0, sem).wait()
    pltpu.make_async_copy(ys_hbm.at[pl.ds(0, tm)], b1, sem).wait()
    g = gate_ref[...]
    y = g[:, 0:1] * b0[...] + g[:, 1:2] * b1[...]
    x2 = x1_ref[...] + g2_ref[...] * y.reshape(bb, tt, D)
    if final_norm:
        x2 = x2 * lax.rsqrt(jnp.mean(x2 * x2, axis=-1, keepdims=True) + EPS) * fg_ref[...]
    o_ref[...] = x2


def _combine(dest_tiles, ys, gates, x1, mod, l, fgain, final_norm):
    B, T, D = x1.shape
    bb, tt = _tok_tiles(B, T)
    nt = T // tt
    tm = bb * tt
    ns = D // LANES
    return pl.pallas_call(
        functools.partial(_combine_kernel, final_norm=final_norm), name="combine", grid=(B // bb, nt),
        in_specs=[pl.BlockSpec(memory_space=pl.ANY), pl.BlockSpec(memory_space=pl.ANY),
                  pl.BlockSpec((tm, LANES), lambda b, t: (b * nt + t, 0)),
                  pl.BlockSpec((bb, tt, D), lambda b, t: (b, t, 0)),
                  _mod_spec(l, 5, bb, D), pl.BlockSpec((1, D), lambda b, t: (0, 0))],
        out_specs=pl.BlockSpec((bb, tt, D), lambda b, t: (b, t, 0)),
        out_shape=jax.ShapeDtypeStruct((B, T, D), F32),
        scratch_shapes=[pltpu.SMEM((2 * tm,), jnp.int32), pltpu.VMEM((tm, D), F32),
                        pltpu.VMEM((tm, D), F32), pltpu.SemaphoreType.DMA, pltpu.SemaphoreType.DMA],
        compiler_params=_cparams(("arbitrary", "arbitrary")),
    )(dest_tiles, ys, gates, x1, mod, fgain)


def _moe(h2, x1, mod, wr, br, wg, wu, wd, l, jl, fgain, final_norm):
    B, T, D = x1.shape
    bb, tt = _tok_tiles(B, T)
    tm = bb * tt
    N = B * T
    idx, gates, cnt = _router(h2, wr, br, tm, jl)
    counts = cnt[0, :N_EXPERTS].astype(jnp.int32)
    padded = (counts + MOE_ROWS - 1) // MOE_ROWS * MOE_ROWS
    pends = jnp.cumsum(padded)
    pstart = pends - padded
    nblk = -(-(2 * N) // MOE_ROWS) + N_EXPERTS
    starts = jnp.arange(nblk, dtype=jnp.int32) * MOE_ROWS
    block_e = jnp.minimum(jnp.sum((pends[None, :] <= starts[:, None]).astype(jnp.int32), axis=1), N_EXPERTS - 1)
    n_used = (pends[-1:] // MOE_ROWS).astype(jnp.int32)
    pstart_vec = jnp.zeros((1, LANES), F32).at[0, :N_EXPERTS].set(pstart.astype(F32))
    dest = _slots(idx, pstart_vec, tm)
    dest_tiles = dest[:, :2].reshape(N // tm, tm, 2).transpose(0, 2, 1).reshape(N // tm, 2 * tm)
    rows = nblk * MOE_ROWS
    pad_lo = jnp.concatenate([pstart + counts, pends[-1:]])
    pad_hi = jnp.concatenate([pends, jnp.full((1,), rows, jnp.int32)])
    xs = _dispatch(pad_lo, pad_hi, dest_tiles, h2, rows, tm)
    ys = _experts(block_e, n_used, xs, wg, wu, wd, jl)
    return _combine(dest_tiles, ys, gates, x1, mod, l, fgain, final_norm)


def _final_kernel(x_ref, g_ref, o_ref):
    x = x_ref[...]
    o_ref[...] = x * lax.rsqrt(jnp.mean(x * x, axis=-1, keepdims=True) + EPS) * g_ref[...]


def _final(x, g):
    B, T, D = x.shape
    bb, tt = _tok_tiles(B, T)
    return pl.pallas_call(
        _final_kernel, name="final", grid=(B // bb, T // tt),
        in_specs=[pl.BlockSpec((bb, tt, D), lambda b, t: (b, t, 0)), pl.BlockSpec((1, D), lambda b, t: (0, 0))],
        out_specs=pl.BlockSpec((bb, tt, D), lambda b, t: (b, t, 0)),
        out_shape=jax.ShapeDtypeStruct((B, T, D), F32),
        compiler_params=_cparams(("parallel", "parallel")),
    )(x, g)


def _trunk(x, mods, st, p):
    L = mods.shape[0]
    new = [[] for _ in range(5)]
    h0_all = st[4][:, :, None, :]
    nd_stack = jnp.zeros(st[2].shape, F32)
    for l in range(L):
        dense = l % 2 == 0
        j = l // 2
        u, qkv, z, lru, ba = _inproj(x, mods, p['norm1'], p['w_in'], p['w_in_tail'], l)
        ya, n_a = _mixa(u, st[0], p['conv_a_w'], p['conv_a_b'], p['ln_a_g'], p['ln_a_b'], l)
        yb, n_q, nd_stack = _gdn(qkv, z, ba, st[1], st[2], p['conv_qkv_w'], p['alog'], p['dtb'], p['gdn_norm'],
                                 nd_stack, l)
        yc, n_cl, n_h = _lru(lru, st[3], h0_all, p['conv_lru_w'], p['conv_lru_b'], p['lru_wa'], p['lru_b_a'],
                             p['lru_wi'], p['lru_b_i'], p['lru_lambda'], p['lru_norm'], l)
        x1, h2 = _outproj(ya, yb, yc, x, mods, p['norm2'], p['w_out'], l, tile_out=not dense)
        if dense:
            x = _ffn(h2, x1, mods, p['ffn_wg'], p['ffn_wu'], p['ffn_wd'], l, j)
        else:
            x = _moe(h2, x1, mods, p['router_w'], p['router_b'], p['moe_wg'], p['moe_wu'], p['moe_wd'], l, j,
                     p['final_norm'], l == L - 1)
        for lst, val in zip(new, (n_a, n_q, None, n_cl, n_h[:, 0, :])):
            lst.append(val)
    y = x if (L - 1) % 2 == 1 else _final(x, p['final_norm'])
    return y, [nd_stack if k == 2 else jnp.stack(v, axis=0) for k, v in enumerate(new)]


def kernel(x_prompt, x_sample, c_prompt, c_sample, state_conv_a, state_conv_qkv, state_delta, state_conv_lru, state_lru, w_ada, b_ada, norm1, norm2, w_in, w_out, conv_a_w, conv_a_b, ln_a_g, ln_a_b, conv_qkv_w, gdn_a_log, gdn_dt_bias, gdn_norm, conv_lru_w, conv_lru_b, lru_w_a, lru_b_a, lru_w_i, lru_b_i, lru_lambda, lru_norm, ffn_wg, ffn_wu, ffn_wd, router_w, router_b, moe_wg, moe_wu, moe_wd, final_norm):
    L, D = norm1.shape
    Bp, Bs = x_prompt.shape[0], x_sample.shape[0]
    nh = gdn_a_log.shape[1]

    def row(v):
        return v[:, None, :]

    def lanes(v):
        return jnp.zeros((L, 1, LANES), F32).at[:, 0, nh:2 * nh].set(v)

    def block_diag(w):
        n, k = w.shape[1], w.shape[2]
        eye = jnp.eye(n, dtype=w.dtype)
        return jnp.einsum('lnij,nm->lnimj', w, eye).reshape(L, n * k, n * k)

    w_in_tail = jnp.concatenate([w_in[:, :, _IN_HEAD + 2 * nh:], w_in[:, :, _IN_HEAD:_IN_HEAD + 2 * nh],
                                 jnp.zeros((L, D, LANES - 2 * nh), w_in.dtype)], axis=-1).astype(BF16)
    n_moe = router_w.shape[0]
    p = dict(
        norm1=row(norm1), norm2=row(norm2), w_in=w_in.astype(BF16), w_in_tail=w_in_tail,
        w_out=w_out.astype(BF16),
        conv_a_w=conv_a_w, conv_a_b=row(conv_a_b), ln_a_g=row(ln_a_g), ln_a_b=row(ln_a_b),
        conv_qkv_w=conv_qkv_w, alog=lanes(gdn_a_log), dtb=lanes(gdn_dt_bias), gdn_norm=row(gdn_norm),
        conv_lru_w=conv_lru_w, conv_lru_b=row(conv_lru_b), lru_wa=block_diag(lru_w_a), lru_b_a=row(lru_b_a),
        lru_wi=block_diag(lru_w_i), lru_b_i=row(lru_b_i), lru_lambda=row(lru_lambda), lru_norm=row(lru_norm),
        ffn_wg=ffn_wg.astype(BF16), ffn_wu=ffn_wu.astype(BF16), ffn_wd=ffn_wd.astype(BF16),
        router_w=jnp.concatenate([router_w, jnp.zeros((n_moe, D, LANES - N_EXPERTS), F32)], axis=-1),
        router_b=jnp.concatenate([router_b, jnp.full((n_moe, LANES - N_EXPERTS), NEG_BIG, F32)],
                                 axis=-1)[:, None, :],
        moe_wg=moe_wg.astype(BF16), moe_wu=moe_wu.astype(BF16), moe_wd=moe_wd.astype(BF16),
        final_norm=final_norm[None, :],
    )

    c_all = jnp.concatenate([c_prompt, c_sample], axis=0)
    mod = _ada(c_all, w_ada, b_ada)
    mod = mod.reshape(L, Bp + Bs, 6, D).transpose(0, 2, 1, 3)[:, :, :, None, :]
    mod_p, mod_s = mod[:, :, :Bp], mod[:, :, Bp:]

    zero_states = [jnp.zeros((L, Bp) + s.shape[2:], F32)
                   for s in (state_conv_a, state_conv_qkv, state_delta, state_conv_lru, state_lru)]
    y_p, st_p = _trunk(x_prompt, mod_p, zero_states, p)
    y_s, st_s = _trunk(x_sample, mod_s,
                       [state_conv_a, state_conv_qkv, state_delta, state_conv_lru, state_lru], p)
    return (y_p, y_s, *st_p, *st_s)
```
